```python
import math
import jax, jax.numpy as jnp
from jax import lax
import numpy as np

D_MODEL = 2048
BATCH = 8
SEQ = 4096
DEPTH = 4

H_A = 8
DQK_A = 128
DV_A = 256
D_A = H_A * DV_A
QK_A = H_A * DQK_A
CHUNK = 64
CONV_K = 4
H_B = 16
D_NOPE = 128
D_ROPE = 64
D_HQK = D_NOPE + D_ROPE
D_VB = 128
D_B = H_B * D_VB
Q_LORA = 512
KV_LORA = 512
Q_BLOCK = 128
ROPE_THETA = 10000.0
EPS = 1e-6
NEG_INF = -1e30
SPLIT_SIZES = (QK_A, QK_A, D_A, D_A, H_A, H_A, D_A, Q_LORA, KV_LORA, D_ROPE, D_B, D_MODEL, D_MODEL)
N_IN = sum(SPLIT_SIZES)

kernel_name = "hybrid_mlstm_mla_gated_block"


def rms_norm(x, g):
    x32 = x.astype(jnp.float32)
    y = x32 * lax.rsqrt(jnp.mean(x32 * x32, axis=-1, keepdims=True) + EPS)
    return (y * g.astype(jnp.float32)).astype(x.dtype)


def causal_depthwise_conv(x, w):
    s_len = x.shape[1]
    xp = jnp.pad(x, ((0, 0), (CONV_K - 1, 0), (0, 0)))
    out = xp[:, CONV_K - 1:] * w[CONV_K - 1]
    for j in range(CONV_K - 1):
        out = out + xp[:, j:j + s_len] * w[j]
    return out


def rope_tables(positions):
    inv_freq = jnp.exp(-math.log(ROPE_THETA) * jnp.arange(0, D_ROPE, 2, dtype=jnp.float32) / D_ROPE)
    ang = positions.astype(jnp.float32)[..., None] * inv_freq
    return jnp.cos(ang)[:, :, None, :], jnp.sin(ang)[:, :, None, :]


def apply_rope(x, cos, sin):
    x32 = x.astype(jnp.float32)
    x1, x2 = jnp.split(x32, 2, axis=-1)
    return jnp.concatenate([x1 * cos - x2 * sin, x2 * cos + x1 * sin], axis=-1).astype(x.dtype)


def mlstm_chunkwise(q, k, v, i_pre, f_pre):
    b_sz, s_len, n_h, dk = q.shape
    dv = v.shape[-1]
    n_chunks = s_len // CHUNK
    f32 = jnp.float32

    def chunks(t):
        t = t.astype(f32).reshape((b_sz, n_chunks, CHUNK, n_h) + t.shape[3:])
        return jnp.moveaxis(t, (1, 3), (0, 2))

    qc_all = chunks(q) * (dk ** -0.5)
    kc_all = chunks(k)
    vc_all = chunks(v)
    ic_all = chunks(i_pre)
    lf_all = jax.nn.log_sigmoid(chunks(f_pre))
    causal = jnp.tril(jnp.ones((CHUNK, CHUNK), dtype=bool))

    def step(carry, xs):
        c_st, n_st, m_st = carry
        qc, kc, vc, ic, fc = xs
        b = jnp.cumsum(fc, axis=-1)
        g = b[..., -1]
        d = jnp.where(causal, b[..., :, None] - b[..., None, :] + ic[..., None, :], NEG_INF)
        inter = b + m_st[..., None]
        m_i = jnp.maximum(jnp.max(d, axis=-1), inter)
        s = jnp.einsum('bhid,bhjd->bhij', qc, kc) * jnp.exp(d - m_i[..., None])
        decay = jnp.exp(inter - m_i)
        num = jnp.einsum('bhij,bhje->bhie', s, vc) + decay[..., None] * jnp.einsum('bhid,bhde->bhie', qc, c_st)
        den = jnp.sum(s, axis=-1) + decay * jnp.einsum('bhid,bhd->bhi', qc, n_st)
        h = num / jnp.maximum(jnp.abs(den), jnp.exp(-m_i))[..., None]
        w = g[..., None] - b + ic
        m_new = jnp.maximum(g + m_st, jnp.max(w, axis=-1))
        carry_scale = jnp.exp(g + m_st - m_new)
        kw = kc * jnp.exp(w - m_new[..., None])[..., None]
        c_new = carry_scale[..., None, None] * c_st + jnp.einsum('bhjd,bhje->bhde', kw, vc)
        n_new = carry_scale[..., None] * n_st + jnp.sum(kw, axis=-2)
        return (c_new, n_new, m_new), h

    init = (jnp.zeros((b_sz, n_h, dk, dv), f32), jnp.zeros((b_sz, n_h, dk), f32), jnp.zeros((b_sz, n_h), f32))
    _, h_all = lax.scan(step, init, (qc_all, kc_all, vc_all, ic_all, lf_all))
    h_all = jnp.moveaxis(h_all, (0, 2), (1, 3)).reshape(b_sz, s_len, n_h, dv)
    return h_all.astype(v.dtype)


def causal_attention(q, k, v):
    b_sz, s_len, n_h, dq = q.shape
    n_blocks = s_len // Q_BLOCK
    scale = dq ** -0.5
    key_idx = jnp.arange(s_len)
    q_blocks = jnp.moveaxis(q.reshape(b_sz, n_blocks, Q_BLOCK, n_h, dq), 1, 0)

    def one_block(args):
        q_blk, blk_id = args
        q_idx = blk_id * Q_BLOCK + jnp.arange(Q_BLOCK)
        s = jnp.einsum('bqhd,bkhd->bhqk', q_blk, k).astype(jnp.float32) * scale
        s = jnp.where(key_idx[None, :] <= q_idx[:, None], s, NEG_INF)
        p = jax.nn.softmax(s, axis=-1).astype(v.dtype)
        return jnp.einsum('bhqk,bkhd->bqhd', p, v)

    o = lax.map(one_block, (q_blocks, jnp.arange(n_blocks)))
    return jnp.moveaxis(o, 0, 1).reshape(b_sz, s_len, n_h, v.shape[-1])


def hybrid_layer(x, cos, sin, norm_g, w_in, gate_bias, conv_w, mlstm_norm_g, w_a,
                 q_lat_g, kv_lat_g, w_uq, w_ukv, q_norm_g, k_norm_g, w_b, w_out):
    b_sz, s_len, _ = x.shape
    h = rms_norm(x, norm_g)
    proj = jnp.einsum('bsd,dn->bsn', h, w_in)
    offsets = [int(o) for o in np.cumsum(SPLIT_SIZES)[:-1]]
    (q_a, k_a, v_a, o_a, i_a, f_a, z_a, c_q, c_kv, k_r, z_b, g_a, g_b) = jnp.split(proj, offsets, axis=-1)

    qk_a = jax.nn.silu(causal_depthwise_conv(jnp.concatenate([q_a, k_a], axis=-1), conv_w))
    q_a, k_a = jnp.split(qk_a, 2, axis=-1)
    h_a = mlstm_chunkwise(q_a.reshape(b_sz, s_len, H_A, DQK_A),
                          k_a.reshape(b_sz, s_len, H_A, DQK_A),
                          v_a.reshape(b_sz, s_len, H_A, DV_A),
                          i_a + gate_bias[:H_A], f_a + gate_bias[H_A:])
    h_a = rms_norm(h_a, mlstm_norm_g.reshape(H_A, DV_A)).reshape(b_sz, s_len, D_A)
    y_a = jnp.einsum('bsc,cd->bsd', jax.nn.sigmoid(o_a) * h_a * jax.nn.silu(z_a), w_a)

    q_b = jnp.einsum('bsr,rn->bsn', rms_norm(c_q, q_lat_g), w_uq).reshape(b_sz, s_len, H_B, D_HQK)
    kv_b = jnp.einsum('bsr,rn->bsn', rms_norm(c_kv, kv_lat_g), w_ukv).reshape(b_sz, s_len, H_B, D_NOPE + D_VB)
    k_nope, v_b = jnp.split(kv_b, [D_NOPE], axis=-1)
    k_b = jnp.concatenate([k_nope, jnp.broadcast_to(k_r[:, :, None, :], (b_sz, s_len, H_B, D_ROPE))], axis=-1)
    q_b = rms_norm(q_b, q_norm_g)
    k_b = rms_norm(k_b, k_norm_g)
    q_b = jnp.concatenate([q_b[..., :D_NOPE], apply_rope(q_b[..., D_NOPE:], cos, sin)], axis=-1)
    k_b = jnp.concatenate([k_b[..., :D_NOPE], apply_rope(k_b[..., D_NOPE:], cos, sin)], axis=-1)
    o_b = causal_attention(q_b, k_b, v_b).reshape(b_sz, s_len, D_B)
    y_b = jnp.einsum('bsc,cd->bsd', o_b * jax.nn.silu(z_b), w_b)

    y = jax.nn.sigmoid(g_a) * y_a + jax.nn.sigmoid(g_b) * y_b
    return x + jnp.einsum('bsd,de->bse', y, w_out)


def setup_inputs(seed: int = 0) -> dict:
    key = jax.random.key(seed)
    ks = jax.random.split(key, 16)
    f32 = jnp.float32

    def nrm(k, shape, scale):
        return jax.random.normal(k, shape, f32) * scale

    x = nrm(ks[0], (BATCH, SEQ, D_MODEL), 1.0)
    positions = (jax.random.randint(ks[1], (BATCH, 1), 0, 1024, dtype=jnp.int32)
                 + jnp.arange(SEQ, dtype=jnp.int32)[None, :])
    norm_g = 1.0 + nrm(ks[2], (DEPTH, D_MODEL), 0.02)
    w_in = nrm(ks[3], (DEPTH, D_MODEL, N_IN), D_MODEL ** -0.5)
    kb1, kb2 = jax.random.split(ks[4])
    gate_bias = jnp.concatenate([nrm(kb1, (DEPTH, H_A), 0.1),
                                 jnp.linspace(3.0, 6.0, H_A, dtype=f32)[None, :] + nrm(kb2, (DEPTH, H_A), 0.1)], axis=-1)
    conv_w = nrm(ks[5], (DEPTH, CONV_K, 2 * QK_A), CONV_K ** -0.5)
    mlstm_norm_g = 1.0 + nrm(ks[6], (DEPTH, D_A), 0.02)
    w_a = nrm(ks[7], (DEPTH, D_A, D_MODEL), D_A ** -0.5)
    q_lat_g = 1.0 + nrm(ks[8], (DEPTH, Q_LORA), 0.02)
    kv_lat_g = 1.0 + nrm(ks[9], (DEPTH, KV_LORA), 0.02)
    w_uq = nrm(ks[10], (DEPTH, Q_LORA, H_B * D_HQK), Q_LORA ** -0.5)
    w_ukv = nrm(ks[11], (DEPTH, KV_LORA, H_B * (D_NOPE + D_VB)), KV_LORA ** -0.5)
    q_norm_g = 1.0 + nrm(ks[12], (DEPTH, D_HQK), 0.02)
    k_norm_g = 1.0 + nrm(ks[13], (DEPTH, D_HQK), 0.02)
    w_b = nrm(ks[14], (DEPTH, D_B, D_MODEL), D_B ** -0.5)
    w_out = nrm(ks[15], (DEPTH, D_MODEL, D_MODEL), D_MODEL ** -0.5)
    return {"x": x, "positions": positions, "norm_g": norm_g, "w_in": w_in, "gate_bias": gate_bias,
            "conv_w": conv_w, "mlstm_norm_g": mlstm_norm_g, "w_a": w_a, "q_lat_g": q_lat_g,
            "kv_lat_g": kv_lat_g, "w_uq": w_uq, "w_ukv": w_ukv, "q_norm_g": q_norm_g,
            "k_norm_g": k_norm_g, "w_b": w_b, "w_out": w_out}


def reference(x, positions, norm_g, w_in, gate_bias, conv_w, mlstm_norm_g, w_a, q_lat_g,
              kv_lat_g, w_uq, w_ukv, q_norm_g, k_norm_g, w_b, w_out):
    cos, sin = rope_tables(positions)
    for l in range(DEPTH):
        x = hybrid_layer(x, cos, sin, norm_g[l], w_in[l], gate_bias[l], conv_w[l], mlstm_norm_g[l],
                         w_a[l], q_lat_g[l], kv_lat_g[l], w_uq[l], w_ukv[l], q_norm_g[l],
                         k_norm_g[l], w_b[l], w_out[l])
    return x
```

```python
import functools
import math

import jax
import jax.numpy as jnp
from jax import lax
from jax.experimental import pallas as pl
from jax.experimental.pallas import tpu as pltpu

F32 = jnp.float32
BF16 = jnp.bfloat16

DQK_A = 128
DV_A = 256
CONV_K = 4
D_NOPE = 128
D_ROPE = 64
D_HQK = D_NOPE + D_ROPE
D_VB = 128
ROPE_THETA = 10000.0
EPS = 1e-6
NEG_INF = -1e30

LANES = 128
VMEM_LIMIT_BYTES = 56 * 1024 * 1024

MLSTM_CHUNK = 256
ATTN_BLOCK = 256
PREP_ROWS = 256
NORM_ROWS = 256


def _pick(n, candidates):
    for c in candidates:
        if n % c == 0:
            return c
    raise ValueError(f"no tile in {candidates} divides {n}")


def _cparams(*sem):
    return pltpu.CompilerParams(dimension_semantics=sem, vmem_limit_bytes=VMEM_LIMIT_BYTES)


def _sigmoid(x):
    return 1.0 / (1.0 + jnp.exp(-x))


def _silu(x):
    return x * _sigmoid(x)


def _log_sigmoid(x):
    return jnp.minimum(x, 0.0) - jnp.log(1.0 + jnp.exp(-jnp.abs(x)))


def _dot_nt(a, b):
    return lax.dot_general(a, b, (((1,), (1,)), ((), ())), preferred_element_type=F32)


def _dot_tn(a, b):
    return lax.dot_general(a, b, (((0,), (0,)), ((), ())), preferred_element_type=F32)


def _inproj_kernel(x_ref, g_ref, w_ref, ws_ref, wgt_ref, o_ref, os_ref, ogt_ref, h_scr):
    @pl.when(pl.program_id(1) == 0)
    def _():
        rows = x_ref.shape[0]
        step = min(NORM_ROWS, rows)
        for r0 in range(0, rows, step):
            x = x_ref[r0:r0 + step, :]
            ms = jnp.mean(x * x, axis=-1, keepdims=True)
            h_scr[r0:r0 + step, :] = (x * lax.rsqrt(ms + EPS) * g_ref[...]).astype(BF16)
        h = h_scr[...]
        os_ref[...] = jnp.dot(h, ws_ref[...], preferred_element_type=F32)
        ogt_ref[...] = _dot_nt(wgt_ref[...], h)

    o_ref[...] = jnp.dot(h_scr[...], w_ref[...], preferred_element_type=F32).astype(BF16)


def _inproj(x2, norm_g, w_main, w_small, w_gt):
    t, d = x2.shape
    nm = w_main.shape[1]
    bm = _pick(t, (1024, 512, 256, 128))
    bn = _pick(nm, (1024, 512, 256, 128))
    return pl.pallas_call(
        _inproj_kernel,
        grid=(t // bm, nm // bn),
        in_specs=[
            pl.BlockSpec((bm, d), lambda i, j: (i, 0)),
            pl.BlockSpec((1, d), lambda i, j: (0, 0)),
            pl.BlockSpec((d, bn), lambda i, j: (0, j)),
            pl.BlockSpec((d, 2 * LANES), lambda i, j: (0, 0)),
            pl.BlockSpec((16, d), lambda i, j: (0, 0)),
        ],
        out_specs=[
            pl.BlockSpec((bm, bn), lambda i, j: (i, j)),
            pl.BlockSpec((bm, 2 * LANES), lambda i, j: (i, 0)),
            pl.BlockSpec((16, bm), lambda i, j: (0, i)),
        ],
        out_shape=[
            jax.ShapeDtypeStruct((t, nm), BF16),
            jax.ShapeDtypeStruct((t, 2 * LANES), F32),
            jax.ShapeDtypeStruct((16, t), F32),
        ],
        scratch_shapes=[pltpu.VMEM((bm, d), BF16)],
        compiler_params=_cparams("parallel", "arbitrary"),
    )(x2, norm_g, w_main, w_small, w_gt)


def _mlstm_kernel(q_ref, k_ref, v_ref, o_ref, z_ref, gc_ref, gr_ref, cw_ref, gbc_ref, gbr_ref, ng_ref,
                  out_ref, c_scr, n_scr, m_scr, qk_scr, *, n_heads, chunk):
    L = chunk
    qk_w = n_heads * DQK_A
    c_idx = pl.program_id(1)

    @pl.when(c_idx == 0)
    def _():
        c_scr[...] = jnp.zeros_like(c_scr)
        n_scr[...] = jnp.zeros_like(n_scr)
        m_scr[...] = jnp.zeros_like(m_scr)
        qk_scr[0:8, :] = jnp.zeros((8, 2 * qk_w), F32)

    @pl.when(c_idx > 0)
    def _():
        qk_scr[0:8, :] = qk_scr[L:L + 8, :]

    qk_scr[8:L + 8, 0:qk_w] = q_ref[...].astype(F32)
    qk_scr[8:L + 8, qk_w:2 * qk_w] = k_ref[...].astype(F32)

    gc = gc_ref[...] + gbc_ref[...]
    gr = gr_ref[...] + gbr_ref[...]
    row = lax.broadcasted_iota(jnp.int32, (L, L), 0)
    col = lax.broadcasted_iota(jnp.int32, (L, L), 1)
    causal = col <= row
    tri = causal.astype(F32)
    tri_t = (row <= col).astype(F32)
    b_c_all = jnp.dot(tri, _log_sigmoid(gc), preferred_element_type=F32, precision=lax.Precision.HIGHEST)
    b_r_all = jnp.dot(_log_sigmoid(gr), tri_t, preferred_element_type=F32, precision=lax.Precision.HIGHEST)

    for h in range(n_heads):
        qs = slice(h * DQK_A, (h + 1) * DQK_A)
        ks = slice(qk_w + h * DQK_A, qk_w + (h + 1) * DQK_A)
        vs = slice(h * DV_A, (h + 1) * DV_A)
        base = 8 - (CONV_K - 1)
        xq = qk_scr[base:base + L, qs] * cw_ref[0:1, qs]
        xk = qk_scr[base:base + L, ks] * cw_ref[0:1, ks]
        for j in range(1, CONV_K):
            xq = xq + qk_scr[base + j:base + j + L, qs] * cw_ref[j:j + 1, qs]
            xk = xk + qk_scr[base + j:base + j + L, ks] * cw_ref[j:j + 1, ks]
        q_f = _silu(xq) * (DQK_A ** -0.5)
        k_f = _silu(xk)
        q_b = q_f.astype(BF16)
        k_b = k_f.astype(BF16)
        v_b = v_ref[:, vs]

        i_c = gc[:, h:h + 1]
        b_c = b_c_all[:, n_heads + h:n_heads + h + 1]
        i_r = gr[h:h + 1, :]
        b_r = b_r_all[n_heads + h:n_heads + h + 1, :]
        m_prev = m_scr[h:h + 1, 0:1]
        g_tot = b_c[L - 1:L, :]

        d_mat = jnp.where(causal, b_c + (i_r - b_r), NEG_INF)
        inter = b_c + m_prev
        m_i = jnp.maximum(jnp.max(d_mat, axis=-1, keepdims=True), inter)
        p = _dot_nt(q_b, k_b) * jnp.exp(d_mat - m_i)
        decay = jnp.exp(inter - m_i)
        c_prev = c_scr[h]
        n_prev = n_scr[h:h + 1, :]
        num = (jnp.dot(p.astype(BF16), v_b, preferred_element_type=F32)
               + decay * jnp.dot(q_b, c_prev.astype(BF16), preferred_element_type=F32))
        den = (jnp.sum(p, axis=-1, keepdims=True)
               + decay * jnp.sum(q_f * n_prev, axis=-1, keepdims=True))
        hh = num / jnp.maximum(jnp.abs(den), jnp.exp(-m_i))

        w_c = g_tot - b_c + i_c
        m_new = jnp.maximum(g_tot + m_prev, jnp.max(w_c, axis=0, keepdims=True))
        carry_scale = jnp.exp(g_tot + m_prev - m_new)
        kw = k_f * jnp.exp(w_c - m_new)
        c_scr[h] = carry_scale * c_prev + _dot_tn(kw.astype(BF16), v_b)
        n_scr[h:h + 1, :] = carry_scale * n_prev + jnp.sum(kw, axis=0, keepdims=True)
        m_scr[h:h + 1, :] = jnp.broadcast_to(m_new, (1, LANES))

        ms = jnp.mean(hh * hh, axis=-1, keepdims=True)
        hn = hh * lax.rsqrt(ms + EPS) * ng_ref[0:1, vs]
        gated = _sigmoid(o_ref[:, vs].astype(F32)) * hn * _silu(z_ref[:, vs].astype(F32))
        out_ref[:, vs] = gated.astype(BF16)


def _mlstm(proj, small, gates_t, conv_w, gb_col, gb_row, norm_g, *, batch, seq, n_heads, offs):
    t = batch * seq
    L = _pick(seq, (MLSTM_CHUNK, 128, 64))
    nc = seq // L
    qk_w = n_heads * DQK_A
    dv_w = n_heads * DV_A
    assert offs["q"] % qk_w == 0 and offs["k"] % qk_w == 0
    assert offs["v"] % dv_w == 0 and offs["o"] % dv_w == 0 and offs["za"] % dv_w == 0
    qb, kb = offs["q"] // qk_w, offs["k"] // qk_w
    vb, ob, zb = offs["v"] // dv_w, offs["o"] // dv_w, offs["za"] // dv_w
    kern = functools.partial(_mlstm_kernel, n_heads=n_heads, chunk=L)
    return pl.pallas_call(
        kern,
        grid=(batch, nc),
        in_specs=[
            pl.BlockSpec((L, qk_w), lambda b, c: (b * nc + c, qb)),
            pl.BlockSpec((L, qk_w), lambda b, c: (b * nc + c, kb)),
            pl.BlockSpec((L, dv_w), lambda b, c: (b * nc + c, vb)),
            pl.BlockSpec((L, dv_w), lambda b, c: (b * nc + c, ob)),
            pl.BlockSpec((L, dv_w), lambda b, c: (b * nc + c, zb)),
            pl.BlockSpec((L, LANES), lambda b, c: (b * nc + c, 1)),
            pl.BlockSpec((16, L), lambda b, c: (0, b * nc + c)),
            pl.BlockSpec((CONV_K, 2 * qk_w), lambda b, c: (0, 0)),
            pl.BlockSpec((1, LANES), lambda b, c: (0, 0)),
            pl.BlockSpec((16, 1), lambda b, c: (0, 0)),
            pl.BlockSpec((1, dv_w), lambda b, c: (0, 0)),
        ],
        out_specs=pl.BlockSpec((L, dv_w), lambda b, c: (b * nc + c, 0)),
        out_shape=jax.ShapeDtypeStruct((t, dv_w), BF16),
        scratch_shapes=[
            pltpu.VMEM((n_heads, DQK_A, DV_A), F32),
            pltpu.VMEM((8, DQK_A), F32),
            pltpu.VMEM((8, LANES), F32),
            pltpu.VMEM((L + 8, 2 * qk_w), F32),
        ],
        compiler_params=_cparams("parallel", "arbitrary"),
    )(proj, proj, proj, proj, proj, small, gates_t, conv_w, gb_col, gb_row, norm_g)


def _mla_prep_kernel(cq_ref, ckv_ref, kr_ref, cost_ref, sint_ref, cosk_ref, sink_ref, qlg_ref, kvlg_ref,
                     wuqt_ref, wk_ref, wvt_ref, qng_ref, kng_ref, kngs_ref, qt_ref, k_ref, vt_ref, *, n_heads):
    cq = cq_ref[...].astype(F32)
    cqn = (cq * lax.rsqrt(jnp.mean(cq * cq, axis=-1, keepdims=True) + EPS) * qlg_ref[...]).astype(BF16)
    ckv = ckv_ref[...].astype(F32)
    ckvn = (ckv * lax.rsqrt(jnp.mean(ckv * ckv, axis=-1, keepdims=True) + EPS) * kvlg_ref[...]).astype(BF16)

    cos_t = cost_ref[0]
    sin_t = sint_ref[0]
    half = D_ROPE // 2
    q_t = _dot_nt(wuqt_ref[...], cqn)
    for h in range(n_heads):
        qh = q_t[h * D_HQK:(h + 1) * D_HQK, :]
        ss = jnp.sum(qh * qh, axis=0, keepdims=True)
        rs = lax.rsqrt(ss * (1.0 / D_HQK) + EPS) * (D_HQK ** -0.5)
        qn = qh * rs * qng_ref[...]
        x1 = qn[D_NOPE:D_NOPE + half, :]
        x2 = qn[D_NOPE + half:D_HQK, :]
        qt_ref[0, h, 0:D_NOPE, :] = qn[0:D_NOPE, :].astype(BF16)
        qt_ref[0, h, D_NOPE:D_NOPE + half, :] = (x1 * cos_t - x2 * sin_t).astype(BF16)
        qt_ref[0, h, D_NOPE + half:D_HQK, :] = (x2 * cos_t + x1 * sin_t).astype(BF16)

    kn = jnp.dot(ckvn, wk_ref[...], preferred_element_type=F32)
    kr = kr_ref[:, 0:D_ROPE]
    kr_sw = kr_ref[:, D_ROPE:2 * D_ROPE]
    ssr = jnp.sum(kr * kr, axis=-1, keepdims=True)
    k_rot = kr * kng_ref[:, D_NOPE:D_HQK] * cosk_ref[0] + kr_sw * kngs_ref[...] * sink_ref[0]
    for h in range(n_heads):
        kh = kn[:, h * D_NOPE:(h + 1) * D_NOPE]
        ss = jnp.sum(kh * kh, axis=-1, keepdims=True) + ssr
        rs = lax.rsqrt(ss * (1.0 / D_HQK) + EPS)
        k_ref[0, h, :, 0:D_NOPE] = (kh * rs * kng_ref[:, 0:D_NOPE]).astype(BF16)
        k_ref[0, h, :, D_NOPE:D_HQK] = (k_rot * rs).astype(BF16)

    vt_ref[0] = _dot_nt(wvt_ref[...], ckvn).astype(BF16)


def _mla_prep(proj3, small3, cos_t, sin_t, cos_k, sin_k, q_lat_g, kv_lat_g, wuq_t, wk, wv_t, qn_g, kn_g, kn_g_sw,
              *, n_heads, offs):
    batch, seq, _ = proj3.shape
    ql, kvl = wuq_t.shape[1], wk.shape[0]
    ts = _pick(seq, (PREP_ROWS, 128))
    assert offs["cq"] % ql == 0 and offs["ckv"] % kvl == 0
    cqb, ckvb = offs["cq"] // ql, offs["ckv"] // kvl
    half = D_ROPE // 2
    kern = functools.partial(_mla_prep_kernel, n_heads=n_heads)
    const = lambda b, s: (0, 0)
    return pl.pallas_call(
        kern,
        grid=(batch, seq // ts),
        in_specs=[
            pl.BlockSpec((None, ts, ql), lambda b, s: (b, s, cqb)),
            pl.BlockSpec((None, ts, kvl), lambda b, s: (b, s, ckvb)),
            pl.BlockSpec((None, ts, LANES), lambda b, s: (b, s, 0)),
            pl.BlockSpec((1, half, ts), lambda b, s: (b, 0, s)),
            pl.BlockSpec((1, half, ts), lambda b, s: (b, 0, s)),
            pl.BlockSpec((1, ts, D_ROPE), lambda b, s: (b, s, 0)),
            pl.BlockSpec((1, ts, D_ROPE), lambda b, s: (b, s, 0)),
            pl.BlockSpec((1, ql), const),
            pl.BlockSpec((1, kvl), const),
            pl.BlockSpec((n_heads * D_HQK, ql), const),
            pl.BlockSpec((kvl, n_heads * D_NOPE), const),
            pl.BlockSpec((n_heads * D_VB, kvl), const),
            pl.BlockSpec((D_HQK, 1), const),
            pl.BlockSpec((1, D_HQK), const),
            pl.BlockSpec((1, D_ROPE), const),
        ],
        out_specs=[
            pl.BlockSpec((1, n_heads, D_HQK, ts), lambda b, s: (b, 0, 0, s)),
            pl.BlockSpec((1, n_heads, ts, D_HQK), lambda b, s: (b, 0, s, 0)),
            pl.BlockSpec((1, n_heads * D_VB, ts), lambda b, s: (b, 0, s)),
        ],
        out_shape=[
            jax.ShapeDtypeStruct((batch, n_heads, D_HQK, seq), BF16),
            jax.ShapeDtypeStruct((batch, n_heads, seq, D_HQK), BF16),
            jax.ShapeDtypeStruct((batch, n_heads * D_VB, seq), BF16),
        ],
        compiler_params=_cparams("parallel", "parallel"),
    )(proj3, proj3, small3, cos_t, sin_t, cos_k, sin_k, q_lat_g, kv_lat_g, wuq_t, wk, wv_t, qn_g, kn_g, kn_g_sw)


def _attn_kernel(qt_ref, k_ref, vt_ref, z_ref, o_ref, *, blk):
    qi = pl.program_id(2)
    q_t = qt_ref[0, 0]

    def block(j, carry, masked):
        m, l, acc = carry
        start = pl.multiple_of(j * blk, blk)
        kb = k_ref[0, 0, pl.ds(start, blk), :]
        s = jnp.dot(kb, q_t, preferred_element_type=F32)
        if masked:
            kk = lax.broadcasted_iota(jnp.int32, (blk, blk), 0)
            qq = lax.broadcasted_iota(jnp.int32, (blk, blk), 1)
            s = jnp.where(kk <= qq, s, NEG_INF)
        m_new = jnp.maximum(m, jnp.max(s, axis=0, keepdims=True))
        p = jnp.exp(s - m_new)
        alpha = jnp.exp(m - m_new)
        l = alpha * l + jnp.sum(p, axis=0, keepdims=True)
        vb = vt_ref[0, :, pl.ds(start, blk)]
        acc = alpha * acc + jnp.dot(vb, p.astype(BF16), preferred_element_type=F32)
        return m_new, l, acc

    init = (jnp.full((1, blk), NEG_INF, F32), jnp.zeros((1, blk), F32), jnp.zeros((D_VB, blk), F32))
    carry = block(qi, init, True)
    m, l, acc = lax.fori_loop(0, qi, lambda j, c: block(j, c, False), carry)
    o = (acc / l).T
    o_ref[0] = (o * _silu(z_ref[0].astype(F32))).astype(BF16)


def _attention(q_t, k, v_t, proj3, *, offs):
    batch, n_heads, _, seq = q_t.shape
    blk = _pick(seq, (ATTN_BLOCK, 128))
    assert offs["zb"] % D_VB == 0
    zb0 = offs["zb"] // D_VB
    kern = functools.partial(_attn_kernel, blk=blk)
    return pl.pallas_call(
        kern,
        grid=(batch, n_heads, seq // blk),
        in_specs=[
            pl.BlockSpec((1, 1, D_HQK, blk), lambda b, h, i: (b, h, 0, i)),
            pl.BlockSpec((1, 1, seq, D_HQK), lambda b, h, i: (b, h, 0, 0)),
            pl.BlockSpec((1, D_VB, seq), lambda b, h, i: (b, h, 0)),
            pl.BlockSpec((1, blk, D_VB), lambda b, h, i: (b, i, zb0 + h)),
        ],
        out_specs=pl.BlockSpec((1, blk, D_VB), lambda b, h, i: (b, i, h)),
        out_shape=jax.ShapeDtypeStruct((batch, seq, n_heads * D_VB), BF16),
        compiler_params=_cparams("parallel", "parallel", "arbitrary"),
    )(q_t, k, v_t, proj3)


def _merge_kernel(ha_ref, ob_ref, wa_ref, wb_ref, ga_ref, gb_ref, y_ref):
    ya = jnp.dot(ha_ref[...], wa_ref[...], preferred_element_type=F32)
    yb = jnp.dot(ob_ref[...], wb_ref[...], preferred_element_type=F32)
    y = _sigmoid(ga_ref[...].astype(F32)) * ya + _sigmoid(gb_ref[...].astype(F32)) * yb
    y_ref[...] = y.astype(BF16)


def _merge(ha, ob, w_a, w_b, proj, *, offs):
    t, da = ha.shape
    db = ob.shape[1]
    d = w_a.shape[1]
    bm = _pick(t, (512, 256, 128))
    bn = _pick(d, (1024, 512, 256, 128))
    assert offs["ga"] % bn == 0 and offs["gb"] % bn == 0
    ga0, gb0 = offs["ga"] // bn, offs["gb"] // bn
    return pl.pallas_call(
        _merge_kernel,
        grid=(t // bm, d // bn),
        in_specs=[
            pl.BlockSpec((bm, da), lambda i, j: (i, 0)),
            pl.BlockSpec((bm, db), lambda i, j: (i, 0)),
            pl.BlockSpec((da, bn), lambda i, j: (0, j)),
            pl.BlockSpec((db, bn), lambda i, j: (0, j)),
            pl.BlockSpec((bm, bn), lambda i, j: (i, ga0 + j)),
            pl.BlockSpec((bm, bn), lambda i, j: (i, gb0 + j)),
        ],
        out_specs=pl.BlockSpec((bm, bn), lambda i, j: (i, j)),
        out_shape=jax.ShapeDtypeStruct((t, d), BF16),
        compiler_params=_cparams("parallel", "arbitrary"),
    )(ha, ob, w_a, w_b, proj, proj)


def _outproj_kernel(y_ref, w_ref, x_ref, o_ref):
    o_ref[...] = x_ref[...] + jnp.dot(y_ref[...], w_ref[...], preferred_element_type=F32)


def _outproj(y, w_out, x2):
    t, d = y.shape
    bm = _pick(t, (1024, 512, 256, 128))
    bn = _pick(d, (1024, 512, 256, 128))
    return pl.pallas_call(
        _outproj_kernel,
        grid=(t // bm, d // bn),
        in_specs=[
            pl.BlockSpec((bm, d), lambda i, j: (i, 0)),
            pl.BlockSpec((d, bn), lambda i, j: (0, j)),
            pl.BlockSpec((bm, bn), lambda i, j: (i, j)),
        ],
        out_specs=pl.BlockSpec((bm, bn), lambda i, j: (i, j)),
        out_shape=jax.ShapeDtypeStruct((t, d), F32),
        compiler_params=_cparams("parallel", "arbitrary"),
    )(y, w_out, x2)


def kernel(x, positions, norm_g, w_in, gate_bias, conv_w, mlstm_norm_g, w_a, q_lat_g, kv_lat_g, w_uq, w_ukv,
           q_norm_g, k_norm_g, w_b, w_out):
    batch, seq, d = x.shape
    depth = w_in.shape[0]
    h_a = gate_bias.shape[-1] // 2
    h_b = w_uq.shape[-1] // D_HQK
    qk_w, d_a, d_b = h_a * DQK_A, h_a * DV_A, h_b * D_VB
    ql, kvl = w_uq.shape[1], w_ukv.shape[1]
    assert 2 * h_a <= 16 and w_ukv.shape[-1] == h_b * (D_NOPE + D_VB)

    sizes = (qk_w, qk_w, d_a, d_a, h_a, h_a, d_a, ql, kvl, D_ROPE, d_b, d, d)
    names = ("q", "k", "v", "o", "i", "f", "za", "cq", "ckv", "kr", "zb", "ga", "gb")
    src, acc = {}, 0
    for nme, sz in zip(names, sizes):
        src[nme] = (acc, acc + sz)
        acc += sz
    assert acc == w_in.shape[-1]

    order = ("q", "k", "v", "o", "za", "zb", "ga", "gb", "cq", "ckv")
    offs, acc = {}, 0
    for nme in order:
        offs[nme] = acc
        acc += src[nme][1] - src[nme][0]
    w_main = jnp.concatenate([w_in[:, :, src[n][0]:src[n][1]] for n in order], axis=-1).astype(BF16)

    half = D_ROPE // 2
    w_kr = w_in[:, :, src["kr"][0]:src["kr"][1]]
    w_kr_sw = jnp.concatenate([w_kr[..., half:], w_kr[..., :half]], axis=-1)
    w_if = w_in[:, :, src["i"][0]:src["f"][1]]
    pad_c = jnp.zeros((depth, d, LANES - 2 * h_a), w_in.dtype)
    w_small = jnp.concatenate([w_kr, w_kr_sw, w_if, pad_c], axis=-1).astype(BF16)
    w_gt = jnp.concatenate([jnp.swapaxes(w_if, 1, 2), jnp.zeros((depth, 16 - 2 * h_a, d), w_in.dtype)],
                           axis=1).astype(BF16)

    gb_col = jnp.pad(gate_bias, ((0, 0), (0, LANES - 2 * h_a)))[:, None, :]
    gb_row = jnp.pad(gate_bias, ((0, 0), (0, 16 - 2 * h_a)))[:, :, None]

    w_a_b, w_b_b, w_out_b = w_a.astype(BF16), w_b.astype(BF16), w_out.astype(BF16)
    wuq_t = jnp.swapaxes(w_uq, 1, 2).astype(BF16)
    w_ukv4 = w_ukv.reshape(depth, kvl, h_b, D_NOPE + D_VB)
    wk = w_ukv4[..., :D_NOPE].reshape(depth, kvl, h_b * D_NOPE).astype(BF16)
    wv_t = jnp.swapaxes(w_ukv4[..., D_NOPE:].reshape(depth, kvl, h_b * D_VB), 1, 2).astype(BF16)
    qn_g = q_norm_g[:, :, None]
    kn_g = k_norm_g[:, None, :]
    kn_g_sw = jnp.concatenate([k_norm_g[:, D_NOPE + half:], k_norm_g[:, D_NOPE:D_NOPE + half]], axis=-1)[:, None, :]

    inv_freq = jnp.exp(-math.log(ROPE_THETA) * jnp.arange(0, D_ROPE, 2, dtype=F32) / D_ROPE)
    ang = positions.astype(F32)[..., None] * inv_freq
    cos, sin = jnp.cos(ang), jnp.sin(ang)
    cos_t, sin_t = jnp.swapaxes(cos, 1, 2), jnp.swapaxes(sin, 1, 2)
    cos_k = jnp.concatenate([cos, cos], axis=-1)
    sin_k = jnp.concatenate([-sin, sin], axis=-1)

    x2 = x.reshape(batch * seq, d)
    for l in range(depth):
        proj, small, gates_t = _inproj(x2, norm_g[l][None, :], w_main[l], w_small[l], w_gt[l])
        h_gated = _mlstm(proj, small, gates_t, conv_w[l], gb_col[l], gb_row[l], mlstm_norm_g[l][None, :],
                         batch=batch, seq=seq, n_heads=h_a, offs=offs)
        proj3 = proj.reshape(batch, seq, -1)
        small3 = small.reshape(batch, seq, -1)
        q_t, k, v_t = _mla_prep(proj3, small3, cos_t, sin_t, cos_k, sin_k, q_lat_g[l][None, :], kv_lat_g[l][None, :],
                                wuq_t[l], wk[l], wv_t[l], qn_g[l], kn_g[l], kn_g_sw[l], n_heads=h_b, offs=offs)
        o_gated = _attention(q_t, k, v_t, proj3, offs=offs).reshape(batch * seq, d_b)
        y = _merge(h_gated, o_gated, w_a_b[l], w_b_b[l], proj, offs=offs)
        x2 = _outproj(y, w_out_b[l], x2)
    return x2.reshape(batch, seq, d)
```

```python
import functools
import math

import jax
import jax.numpy as jnp
from jax import lax
from jax.experimental import pallas as pl
from jax.experimental.pallas import tpu as pltpu

F32 = jnp.float32
BF16 = jnp.bfloat16

DQK_A = 128
DV_A = 256
CONV_K = 4
D_NOPE = 128
D_ROPE = 64
D_HQK = D_NOPE + D_ROPE
D_VB = 128
ROPE_THETA = 10000.0
EPS = 1e-6
NEG_INF = -1e30
LOG2_E = 1.4426950408889634

LANES = 128
VMEM_LIMIT_BYTES = 56 * 1024 * 1024

MLSTM_CHUNK = 256
ATTN_BLOCK = 256
ATTN_SUBBLOCKS = 4
PREP_ROWS = 256
NORM_ROWS = 256


def _pick(n, candidates):
    for c in candidates:
        if n % c == 0:
            return c
    raise ValueError(f"no tile in {candidates} divides {n}")


def _cparams(*sem):
    return pltpu.CompilerParams(dimension_semantics=sem, vmem_limit_bytes=VMEM_LIMIT_BYTES)


def _sigmoid(x):
    return 1.0 / (1.0 + jnp.exp(-x))


def _silu(x):
    return x * _sigmoid(x)


def _log_sigmoid(x):
    return jnp.minimum(x, 0.0) - jnp.log(1.0 + jnp.exp(-jnp.abs(x)))


def _dot_nt(a, b):
    return lax.dot_general(a, b, (((1,), (1,)), ((), ())), preferred_element_type=F32)


def _dot_tn(a, b):
    return lax.dot_general(a, b, (((0,), (0,)), ((), ())), preferred_element_type=F32)


def _inproj_kernel(x_ref, g_ref, w_ref, ws_ref, wgt_ref, o_ref, os_ref, ogt_ref, h_scr):
    @pl.when(pl.program_id(1) == 0)
    def _():
        rows = x_ref.shape[0]
        step = min(NORM_ROWS, rows)
        for r0 in range(0, rows, step):
            x = x_ref[r0:r0 + step, :]
            ms = jnp.mean(x * x, axis=-1, keepdims=True)
            h_scr[r0:r0 + step, :] = (x * lax.rsqrt(ms + EPS) * g_ref[...]).astype(BF16)
        h = h_scr[...]
        os_ref[...] = jnp.dot(h, ws_ref[...], preferred_element_type=F32)
        ogt_ref[...] = _dot_nt(wgt_ref[...], h)

    o_ref[...] = jnp.dot(h_scr[...], w_ref[...], preferred_element_type=F32).astype(BF16)


def _inproj(x2, norm_g, w_main, w_small, w_gt):
    t, d = x2.shape
    nm = w_main.shape[1]
    bm = _pick(t, (1024, 512, 256, 128))
    bn = _pick(nm, (1024, 512, 256, 128))
    return pl.pallas_call(
        _inproj_kernel,
        grid=(t // bm, nm // bn),
        in_specs=[
            pl.BlockSpec((bm, d), lambda i, j: (i, 0)),
            pl.BlockSpec((1, d), lambda i, j: (0, 0)),
            pl.BlockSpec((d, bn), lambda i, j: (0, j)),
            pl.BlockSpec((d, 2 * LANES), lambda i, j: (0, 0)),
            pl.BlockSpec((16, d), lambda i, j: (0, 0)),
        ],
        out_specs=[
            pl.BlockSpec((bm, bn), lambda i, j: (i, j)),
            pl.BlockSpec((bm, 2 * LANES), lambda i, j: (i, 0)),
            pl.BlockSpec((16, bm), lambda i, j: (0, i)),
        ],
        out_shape=[
            jax.ShapeDtypeStruct((t, nm), BF16),
            jax.ShapeDtypeStruct((t, 2 * LANES), F32),
            jax.ShapeDtypeStruct((16, t), F32),
        ],
        scratch_shapes=[pltpu.VMEM((bm, d), BF16)],
        compiler_params=_cparams("parallel", "arbitrary"),
    )(x2, norm_g, w_main, w_small, w_gt)


def _mlstm_kernel(q_ref, k_ref, v_ref, o_ref, z_ref, gc_ref, gr_ref, cw_ref, gbc_ref, gbr_ref, ng_ref,
                  out_ref, c_scr, n_scr, m_scr, qk_scr, *, n_heads, chunk):
    L = chunk
    qk_w = n_heads * DQK_A
    c_idx = pl.program_id(1)

    @pl.when(c_idx == 0)
    def _():
        c_scr[...] = jnp.zeros_like(c_scr)
        n_scr[...] = jnp.zeros_like(n_scr)
        m_scr[...] = jnp.zeros_like(m_scr)
        qk_scr[0:8, :] = jnp.zeros((8, 2 * qk_w), F32)

    @pl.when(c_idx > 0)
    def _():
        qk_scr[0:8, :] = qk_scr[L:L + 8, :]

    qk_scr[8:L + 8, 0:qk_w] = q_ref[...].astype(F32)
    qk_scr[8:L + 8, qk_w:2 * qk_w] = k_ref[...].astype(F32)

    gc = gc_ref[...] + gbc_ref[...]
    gr = gr_ref[...] + gbr_ref[...]
    row = lax.broadcasted_iota(jnp.int32, (L, L), 0)
    col = lax.broadcasted_iota(jnp.int32, (L, L), 1)
    causal = col <= row
    tri = causal.astype(F32)
    tri_t = (row <= col).astype(F32)
    b_c_all = jnp.dot(tri, _log_sigmoid(gc), preferred_element_type=F32, precision=lax.Precision.HIGHEST)
    b_r_all = jnp.dot(_log_sigmoid(gr), tri_t, preferred_element_type=F32, precision=lax.Precision.HIGHEST)

    for h in range(n_heads):
        qs = slice(h * DQK_A, (h + 1) * DQK_A)
        ks = slice(qk_w + h * DQK_A, qk_w + (h + 1) * DQK_A)
        vs = slice(h * DV_A, (h + 1) * DV_A)
        base = 8 - (CONV_K - 1)
        xq = qk_scr[base:base + L, qs] * cw_ref[0:1, qs]
        xk = qk_scr[base:base + L, ks] * cw_ref[0:1, ks]
        for j in range(1, CONV_K):
            xq = xq + qk_scr[base + j:base + j + L, qs] * cw_ref[j:j + 1, qs]
            xk = xk + qk_scr[base + j:base + j + L, ks] * cw_ref[j:j + 1, ks]
        q_f = _silu(xq) * (DQK_A ** -0.5)
        k_f = _silu(xk)
        q_b = q_f.astype(BF16)
        k_b = k_f.astype(BF16)
        v_b = v_ref[:, vs]

        i_c = gc[:, h:h + 1]
        b_c = b_c_all[:, n_heads + h:n_heads + h + 1]
        i_r = gr[h:h + 1, :]
        b_r = b_r_all[n_heads + h:n_heads + h + 1, :]
        m_prev = m_scr[h:h + 1, 0:1]
        g_tot = b_c[L - 1:L, :]

        d_mat = jnp.where(causal, b_c + (i_r - b_r), NEG_INF)
        inter = b_c + m_prev
        m_i = jnp.maximum(jnp.max(d_mat, axis=-1, keepdims=True), inter)
        p = _dot_nt(q_b, k_b) * jnp.exp(d_mat - m_i)
        decay = jnp.exp(inter - m_i)
        c_prev = c_scr[h]
        n_prev = n_scr[h:h + 1, :]
        num = (jnp.dot(p.astype(BF16), v_b, preferred_element_type=F32)
               + decay * jnp.dot(q_b, c_prev.astype(BF16), preferred_element_type=F32))
        den = (jnp.sum(p, axis=-1, keepdims=True)
               + decay * jnp.sum(q_f * n_prev, axis=-1, keepdims=True))
        hh = num / jnp.maximum(jnp.abs(den), jnp.exp(-m_i))

        w_c = g_tot - b_c + i_c
        m_new = jnp.maximum(g_tot + m_prev, jnp.max(w_c, axis=0, keepdims=True))
        carry_scale = jnp.exp(g_tot + m_prev - m_new)
        kw = k_f * jnp.exp(w_c - m_new)
        c_scr[h] = carry_scale * c_prev + _dot_tn(kw.astype(BF16), v_b)
        n_scr[h:h + 1, :] = carry_scale * n_prev + jnp.sum(kw, axis=0, keepdims=True)
        m_scr[h:h + 1, :] = jnp.broadcast_to(m_new, (1, LANES))

        ms = jnp.mean(hh * hh, axis=-1, keepdims=True)
        hn = hh * lax.rsqrt(ms + EPS) * ng_ref[0:1, vs]
        gated = _sigmoid(o_ref[:, vs].astype(F32)) * hn * _silu(z_ref[:, vs].astype(F32))
        out_ref[:, vs] = gated.astype(BF16)


def _mlstm(proj, small, gates_t, conv_w, gb_col, gb_row, norm_g, *, batch, seq, n_heads, offs):
    t = batch * seq
    L = _pick(seq, (MLSTM_CHUNK, 128, 64))
    nc = seq // L
    qk_w = n_heads * DQK_A
    dv_w = n_heads * DV_A
    assert offs["q"] % qk_w == 0 and offs["k"] % qk_w == 0
    assert offs["v"] % dv_w == 0 and offs["o"] % dv_w == 0 and offs["za"] % dv_w == 0
    qb, kb = offs["q"] // qk_w, offs["k"] // qk_w
    vb, ob, zb = offs["v"] // dv_w, offs["o"] // dv_w, offs["za"] // dv_w
    kern = functools.partial(_mlstm_kernel, n_heads=n_heads, chunk=L)
    return pl.pallas_call(
        kern,
        grid=(batch, nc),
        in_specs=[
            pl.BlockSpec((L, qk_w), lambda b, c: (b * nc + c, qb)),
            pl.BlockSpec((L, qk_w), lambda b, c: (b * nc + c, kb)),
            pl.BlockSpec((L, dv_w), lambda b, c: (b * nc + c, vb)),
            pl.BlockSpec((L, dv_w), lambda b, c: (b * nc + c, ob)),
            pl.BlockSpec((L, dv_w), lambda b, c: (b * nc + c, zb)),
            pl.BlockSpec((L, LANES), lambda b, c: (b * nc + c, 1)),
            pl.BlockSpec((16, L), lambda b, c: (0, b * nc + c)),
            pl.BlockSpec((CONV_K, 2 * qk_w), lambda b, c: (0, 0)),
            pl.BlockSpec((1, LANES), lambda b, c: (0, 0)),
            pl.BlockSpec((16, 1), lambda b, c: (0, 0)),
            pl.BlockSpec((1, dv_w), lambda b, c: (0, 0)),
        ],
        out_specs=pl.BlockSpec((L, dv_w), lambda b, c: (b * nc + c, 0)),
        out_shape=jax.ShapeDtypeStruct((t, dv_w), BF16),
        scratch_shapes=[
            pltpu.VMEM((n_heads, DQK_A, DV_A), F32),
            pltpu.VMEM((8, DQK_A), F32),
            pltpu.VMEM((8, LANES), F32),
            pltpu.VMEM((L + 8, 2 * qk_w), F32),
        ],
        compiler_params=_cparams("parallel", "arbitrary"),
    )(proj, proj, proj, proj, proj, small, gates_t, conv_w, gb_col, gb_row, norm_g)


def _mla_prep_kernel(cq_ref, ckv_ref, kr_ref, cost_ref, sint_ref, cosk_ref, sink_ref, qlg_ref, kvlg_ref,
                     wuqt_ref, wk_ref, wvt_ref, qng_ref, kng_ref, kngs_ref, qt_ref, k_ref, vt_ref, *, n_heads):
    cq = cq_ref[...].astype(F32)
    cqn = (cq * lax.rsqrt(jnp.mean(cq * cq, axis=-1, keepdims=True) + EPS) * qlg_ref[...]).astype(BF16)
    ckv = ckv_ref[...].astype(F32)
    ckvn = (ckv * lax.rsqrt(jnp.mean(ckv * ckv, axis=-1, keepdims=True) + EPS) * kvlg_ref[...]).astype(BF16)

    cos_t = cost_ref[0]
    sin_t = sint_ref[0]
    half = D_ROPE // 2
    q_t = _dot_nt(wuqt_ref[...], cqn)
    for h in range(n_heads):
        qh = q_t[h * D_HQK:(h + 1) * D_HQK, :]
        ss = jnp.sum(qh * qh, axis=0, keepdims=True)
        rs = lax.rsqrt(ss * (1.0 / D_HQK) + EPS) * (D_HQK ** -0.5 * LOG2_E)
        qn = qh * rs * qng_ref[...]
        x1 = qn[D_NOPE:D_NOPE + half, :]
        x2 = qn[D_NOPE + half:D_HQK, :]
        qt_ref[0, h, 0:D_NOPE, :] = qn[0:D_NOPE, :].astype(BF16)
        qt_ref[0, h, D_NOPE:D_NOPE + half, :] = (x1 * cos_t - x2 * sin_t).astype(BF16)
        qt_ref[0, h, D_NOPE + half:D_HQK, :] = (x2 * cos_t + x1 * sin_t).astype(BF16)

    kn = jnp.dot(ckvn, wk_ref[...], preferred_element_type=F32)
    kr = kr_ref[:, 0:D_ROPE]
    kr_sw = kr_ref[:, D_ROPE:2 * D_ROPE]
    ssr = jnp.sum(kr * kr, axis=-1, keepdims=True)
    k_rot = kr * kng_ref[:, D_NOPE:D_HQK] * cosk_ref[0] + kr_sw * kngs_ref[...] * sink_ref[0]
    for h in range(n_heads):
        kh = kn[:, h * D_NOPE:(h + 1) * D_NOPE]
        ss = jnp.sum(kh * kh, axis=-1, keepdims=True) + ssr
        rs = lax.rsqrt(ss * (1.0 / D_HQK) + EPS)
        k_ref[0, h, :, 0:D_NOPE] = (kh * rs * kng_ref[:, 0:D_NOPE]).astype(BF16)
        k_ref[0, h, :, D_NOPE:D_HQK] = (k_rot * rs).astype(BF16)

    vt_ref[0] = _dot_nt(wvt_ref[...], ckvn).astype(BF16)


def _mla_prep(proj3, small3, cos_t, sin_t, cos_k, sin_k, q_lat_g, kv_lat_g, wuq_t, wk, wv_t, qn_g, kn_g, kn_g_sw,
              *, n_heads, offs):
    batch, seq, _ = proj3.shape
    ql, kvl = wuq_t.shape[1], wk.shape[0]
    ts = _pick(seq, (PREP_ROWS, 128))
    assert offs["cq"] % ql == 0 and offs["ckv"] % kvl == 0
    cqb, ckvb = offs["cq"] // ql, offs["ckv"] // kvl
    half = D_ROPE // 2
    kern = functools.partial(_mla_prep_kernel, n_heads=n_heads)
    const = lambda b, s: (0, 0)
    return pl.pallas_call(
        kern,
        grid=(batch, seq // ts),
        in_specs=[
            pl.BlockSpec((None, ts, ql), lambda b, s: (b, s, cqb)),
            pl.BlockSpec((None, ts, kvl), lambda b, s: (b, s, ckvb)),
            pl.BlockSpec((None, ts, LANES), lambda b, s: (b, s, 0)),
            pl.BlockSpec((1, half, ts), lambda b, s: (b, 0, s)),
            pl.BlockSpec((1, half, ts), lambda b, s: (b, 0, s)),
            pl.BlockSpec((1, ts, D_ROPE), lambda b, s: (b, s, 0)),
            pl.BlockSpec((1, ts, D_ROPE), lambda b, s: (b, s, 0)),
            pl.BlockSpec((1, ql), const),
            pl.BlockSpec((1, kvl), const),
            pl.BlockSpec((n_heads * D_HQK, ql), const),
            pl.BlockSpec((kvl, n_heads * D_NOPE), const),
            pl.BlockSpec((n_heads * D_VB, kvl), const),
            pl.BlockSpec((D_HQK, 1), const),
            pl.BlockSpec((1, D_HQK), const),
            pl.BlockSpec((1, D_ROPE), const),
        ],
        out_specs=[
            pl.BlockSpec((1, n_heads, D_HQK, ts), lambda b, s: (b, 0, 0, s)),
            pl.BlockSpec((1, n_heads, ts, D_HQK), lambda b, s: (b, 0, s, 0)),
            pl.BlockSpec((1, n_heads * D_VB, ts), lambda b, s: (b, 0, s)),
        ],
        out_shape=[
            jax.ShapeDtypeStruct((batch, n_heads, D_HQK, seq), BF16),
            jax.ShapeDtypeStruct((batch, n_heads, seq, D_HQK), BF16),
            jax.ShapeDtypeStruct((batch, n_heads * D_VB, seq), BF16),
        ],
        compiler_params=_cparams("parallel", "parallel"),
    )(proj3, proj3, small3, cos_t, sin_t, cos_k, sin_k, q_lat_g, kv_lat_g, wuq_t, wk, wv_t, qn_g, kn_g, kn_g_sw)


def _attn_kernel(qt_ref, k_ref, vt_ref, z_ref, o_ref, sa_scr, sb_scr, acc_scr, *, blk, nsub):
    qi = pl.program_id(2)
    base = qi * nsub
    assert nsub % 2 == 0
    every = tuple(range(nsub))

    def scores(j, dst, chains):
        start = pl.multiple_of(j * blk, blk)
        kb = k_ref[0, 0, pl.ds(start, blk), :]
        for c in chains:
            q_t = qt_ref[0, 0, :, c * blk:(c + 1) * blk]
            dst[c] = jnp.dot(kb, q_t, preferred_element_type=F32)

    def consume(j, src, ml, chains, masked_chain):
        start = pl.multiple_of(j * blk, blk)
        vb = vt_ref[0, :, pl.ds(start, blk)]
        ml = list(ml)
        for c in chains:
            m, l = ml[c]
            s = src[c]
            if c == masked_chain:
                kk = lax.broadcasted_iota(jnp.int32, (blk, blk), 0)
                qq = lax.broadcasted_iota(jnp.int32, (blk, blk), 1)
                s = jnp.where(kk <= qq, s, NEG_INF)
            m_new = jnp.maximum(m, jnp.max(s, axis=0, keepdims=True))
            p = jnp.exp2(s - m_new)
            alpha = jnp.exp2(m - m_new)
            l = alpha * l + jnp.sum(p, axis=0, keepdims=True)
            acc_scr[c] = alpha * acc_scr[c] + jnp.dot(vb, p.astype(BF16), preferred_element_type=F32)
            ml[c] = (m_new, l)
        return tuple(ml)

    acc_scr[...] = jnp.zeros_like(acc_scr)
    ml = tuple((jnp.full((1, blk), NEG_INF, F32), jnp.zeros((1, blk), F32)) for _ in every)
    scores(0, sa_scr, every)

    def body(jj, ml):
        j0 = 2 * jj
        scores(j0 + 1, sb_scr, every)
        ml = consume(j0, sa_scr, ml, every, None)
        scores(j0 + 2, sa_scr, every)
        return consume(j0 + 1, sb_scr, ml, every, None)

    ml = lax.fori_loop(0, base // 2, body, ml)

    bufs = (sa_scr, sb_scr)
    for r in range(nsub):
        if r + 1 < nsub:
            scores(base + r + 1, bufs[(r + 1) % 2], tuple(range(r + 1, nsub)))
        ml = consume(base + r, bufs[r % 2], ml, tuple(range(r, nsub)), r)

    for c in every:
        rows = slice(c * blk, (c + 1) * blk)
        o = (acc_scr[c] / ml[c][1]).T
        o_ref[0, rows, :] = (o * _silu(z_ref[0, rows, :].astype(F32))).astype(BF16)


def _attention(q_t, k, v_t, proj3, *, offs):
    batch, n_heads, _, seq = q_t.shape
    blk = _pick(seq, (ATTN_BLOCK, 128))
    tq = _pick(seq, (ATTN_SUBBLOCKS * blk, 2 * blk))
    nsub = tq // blk
    assert offs["zb"] % D_VB == 0
    zb0 = offs["zb"] // D_VB
    kern = functools.partial(_attn_kernel, blk=blk, nsub=nsub)
    return pl.pallas_call(
        kern,
        grid=(batch, n_heads, seq // tq),
        in_specs=[
            pl.BlockSpec((1, 1, D_HQK, tq), lambda b, h, i: (b, h, 0, i)),
            pl.BlockSpec((1, 1, seq, D_HQK), lambda b, h, i: (b, h, 0, 0)),
            pl.BlockSpec((1, D_VB, seq), lambda b, h, i: (b, h, 0)),
            pl.BlockSpec((1, tq, D_VB), lambda b, h, i: (b, i, zb0 + h)),
        ],
        out_specs=pl.BlockSpec((1, tq, D_VB), lambda b, h, i: (b, i, h)),
        out_shape=jax.ShapeDtypeStruct((batch, seq, n_heads * D_VB), BF16),
        scratch_shapes=[
            pltpu.VMEM((nsub, blk, blk), F32),
            pltpu.VMEM((nsub, blk, blk), F32),
            pltpu.VMEM((nsub, D_VB, blk), F32),
        ],
        compiler_params=_cparams("parallel", "parallel", "arbitrary"),
    )(q_t, k, v_t, proj3)


def _merge_kernel(ha_ref, ob_ref, wa_ref, wb_ref, ga_ref, gb_ref, y_ref):
    ya = jnp.dot(ha_ref[...], wa_ref[...], preferred_element_type=F32)
    yb = jnp.dot(ob_ref[...], wb_ref[...], preferred_element_type=F32)
    y = _sigmoid(ga_ref[...].astype(F32)) * ya + _sigmoid(gb_ref[...].astype(F32)) * yb
    y_ref[...] = y.astype(BF16)


def _merge(ha, ob, w_a, w_b, proj, *, offs):
    t, da = ha.shape
    db = ob.shape[1]
    d = w_a.shape[1]
    bm = _pick(t, (512, 256, 128))
    bn = _pick(d, (1024, 512, 256, 128))
    assert offs["ga"] % bn == 0 and offs["gb"] % bn == 0
    ga0, gb0 = offs["ga"] // bn, offs["gb"] // bn
    return pl.pallas_call(
        _merge_kernel,
        grid=(t // bm, d // bn),
        in_specs=[
            pl.BlockSpec((bm, da), lambda i, j: (i, 0)),
            pl.BlockSpec((bm, db), lambda i, j: (i, 0)),
            pl.BlockSpec((da, bn), lambda i, j: (0, j)),
            pl.BlockSpec((db, bn), lambda i, j: (0, j)),
            pl.BlockSpec((bm, bn), lambda i, j: (i, ga0 + j)),
            pl.BlockSpec((bm, bn), lambda i, j: (i, gb0 + j)),
        ],
        out_specs=pl.BlockSpec((bm, bn), lambda i, j: (i, j)),
        out_shape=jax.ShapeDtypeStruct((t, d), BF16),
        compiler_params=_cparams("parallel", "arbitrary"),
    )(ha, ob, w_a, w_b, proj, proj)


def _outproj_kernel(y_ref, w_ref, x_ref, o_ref):
    o_ref[...] = x_ref[...] + jnp.dot(y_ref[...], w_ref[...], preferred_element_type=F32)


def _outproj(y, w_out, x2):
    t, d = y.shape
    bm = _pick(t, (1024, 512, 256, 128))
    bn = _pick(d, (1024, 512, 256, 128))
    return pl.pallas_call(
        _outproj_kernel,
        grid=(t // bm, d // bn),
        in_specs=[
            pl.BlockSpec((bm, d), lambda i, j: (i, 0)),
            pl.BlockSpec((d, bn), lambda i, j: (0, j)),
            pl.BlockSpec((bm, bn), lambda i, j: (i, j)),
        ],
        out_specs=pl.BlockSpec((bm, bn), lambda i, j: (i, j)),
        out_shape=jax.ShapeDtypeStruct((t, d), F32),
        compiler_params=_cparams("parallel", "arbitrary"),
    )(y, w_out, x2)


def kernel(x, positions, norm_g, w_in, gate_bias, conv_w, mlstm_norm_g, w_a, q_lat_g, kv_lat_g, w_uq, w_ukv,
           q_norm_g, k_norm_g, w_b, w_out):
    batch, seq, d = x.shape
    depth = w_in.shape[0]
    h_a = gate_bias.shape[-1] // 2
    h_b = w_uq.shape[-1] // D_HQK
    qk_w, d_a, d_b = h_a * DQK_A, h_a * DV_A, h_b * D_VB
    ql, kvl = w_uq.shape[1], w_ukv.shape[1]
    assert 2 * h_a <= 16 and w_ukv.shape[-1] == h_b * (D_NOPE + D_VB)

    sizes = (qk_w, qk_w, d_a, d_a, h_a, h_a, d_a, ql, kvl, D_ROPE, d_b, d, d)
    names = ("q", "k", "v", "o", "i", "f", "za", "cq", "ckv", "kr", "zb", "ga", "gb")
    src, acc = {}, 0
    for nme, sz in zip(names, sizes):
        src[nme] = (acc, acc + sz)
        acc += sz
    assert acc == w_in.shape[-1]

    order = ("q", "k", "v", "o", "za", "zb", "ga", "gb", "cq", "ckv")
    offs, acc = {}, 0
    for nme in order:
        offs[nme] = acc
        acc += src[nme][1] - src[nme][0]
    w_main = jnp.concatenate([w_in[:, :, src[n][0]:src[n][1]] for n in order], axis=-1).astype(BF16)

    half = D_ROPE // 2
    w_kr = w_in[:, :, src["kr"][0]:src["kr"][1]]
    w_kr_sw = jnp.concatenate([w_kr[..., half:], w_kr[..., :half]], axis=-1)
    w_if = w_in[:, :, src["i"][0]:src["f"][1]]
    pad_c = jnp.zeros((depth, d, LANES - 2 * h_a), w_in.dtype)
    w_small = jnp.concatenate([w_kr, w_kr_sw, w_if, pad_c], axis=-1).astype(BF16)
    w_gt = jnp.concatenate([jnp.swapaxes(w_if, 1, 2), jnp.zeros((depth, 16 - 2 * h_a, d), w_in.dtype)],
                           axis=1).astype(BF16)

    gb_col = jnp.pad(gate_bias, ((0, 0), (0, LANES - 2 * h_a)))[:, None, :]
    gb_row = jnp.pad(gate_bias, ((0, 0), (0, 16 - 2 * h_a)))[:, :, None]

    w_a_b, w_b_b, w_out_b = w_a.astype(BF16), w_b.astype(BF16), w_out.astype(BF16)
    wuq_t = jnp.swapaxes(w_uq, 1, 2).astype(BF16)
    w_ukv4 = w_ukv.reshape(depth, kvl, h_b, D_NOPE + D_VB)
    wk = w_ukv4[..., :D_NOPE].reshape(depth, kvl, h_b * D_NOPE).astype(BF16)
    wv_t = jnp.swapaxes(w_ukv4[..., D_NOPE:].reshape(depth, kvl, h_b * D_VB), 1, 2).astype(BF16)
    qn_g = q_norm_g[:, :, None]
    kn_g = k_norm_g[:, None, :]
    kn_g_sw = jnp.concatenate([k_norm_g[:, D_NOPE + half:], k_norm_g[:, D_NOPE:D_NOPE + half]], axis=-1)[:, None, :]

    inv_freq = jnp.exp(-math.log(ROPE_THETA) * jnp.arange(0, D_ROPE, 2, dtype=F32) / D_ROPE)
    ang = positions.astype(F32)[..., None] * inv_freq
    cos, sin = jnp.cos(ang), jnp.sin(ang)
    cos_t, sin_t = jnp.swapaxes(cos, 1, 2), jnp.swapaxes(sin, 1, 2)
    cos_k = jnp.concatenate([cos, cos], axis=-1)
    sin_k = jnp.concatenate([-sin, sin], axis=-1)

    x2 = x.reshape(batch * seq, d)
    for l in range(depth):
        proj, small, gates_t = _inproj(x2, norm_g[l][None, :], w_main[l], w_small[l], w_gt[l])
        h_gated = _mlstm(proj, small, gates_t, conv_w[l], gb_col[l], gb_row[l], mlstm_norm_g[l][None, :],
                         batch=batch, seq=seq, n_heads=h_a, offs=offs)
        proj3 = proj.reshape(batch, seq, -1)
        small3 = small.reshape(batch, seq, -1)
        q_t, k, v_t = _mla_prep(proj3, small3, cos_t, sin_t, cos_k, sin_k, q_lat_g[l][None, :], kv_lat_g[l][None, :],
                                wuq_t[l], wk[l], wv_t[l], qn_g[l], kn_g[l], kn_g_sw[l], n_heads=h_b, offs=offs)
        o_gated = _attention(q_t, k, v_t, proj3, offs=offs).reshape(batch * seq, d_b)
        y = _merge(h_gated, o_gated, w_a_b[l], w_b_b[l], proj, offs=offs)
        x2 = _outproj(y, w_out_b[l], x2)
    return x2.reshape(batch, seq, d)
```

```python
import functools
import math

import jax
import jax.numpy as jnp
from jax import lax
from jax.experimental import pallas as pl
from jax.experimental.pallas import tpu as pltpu

F32 = jnp.float32
BF16 = jnp.bfloat16

DQK_A = 128
DV_A = 256
CONV_K = 4
D_NOPE = 128
D_ROPE = 64
D_HQK = D_NOPE + D_ROPE
D_VB = 128
ROPE_THETA = 10000.0
EPS = 1e-6
NEG_INF = -1e30
LOG2_E = 1.4426950408889634

LANES = 128
VMEM_LIMIT_BYTES = 56 * 1024 * 1024

MLSTM_CHUNK = 256
CONV_SLAB = 256
ATTN_TILE = 1024
ATTN_SUBBLOCKS = 4
ATTN_PAD_ROWS = 16
PREP_ROWS = 256
NORM_ROWS = 256


def _pick(n, candidates):
    for c in candidates:
        if n % c == 0:
            return c
    raise ValueError(f"no tile in {candidates} divides {n}")


def _cparams(*sem):
    return pltpu.CompilerParams(dimension_semantics=sem, vmem_limit_bytes=VMEM_LIMIT_BYTES)


def _sigmoid(x):
    return 1.0 / (1.0 + jnp.exp2(x * (-LOG2_E)))


def _silu(x):
    return x * _sigmoid(x)


def _log_sigmoid(x):
    return jnp.minimum(x, 0.0) - jnp.log(1.0 + jnp.exp(-jnp.abs(x)))


def _dot_nt(a, b):
    return lax.dot_general(a, b, (((1,), (1,)), ((), ())), preferred_element_type=F32)


def _dot_tn(a, b):
    return lax.dot_general(a, b, (((0,), (0,)), ((), ())), preferred_element_type=F32)


def _inproj_kernel(x_ref, g_ref, w_ref, ws_ref, wgt_ref, o_ref, os_ref, ogt_ref, h_scr):
    @pl.when(pl.program_id(1) == 0)
    def _():
        rows = x_ref.shape[0]
        step = min(NORM_ROWS, rows)
        for r0 in range(0, rows, step):
            x = x_ref[r0:r0 + step, :]
            ms = jnp.mean(x * x, axis=-1, keepdims=True)
            h_scr[r0:r0 + step, :] = (x * lax.rsqrt(ms + EPS) * g_ref[...]).astype(BF16)
        h = h_scr[...]
        os_ref[...] = jnp.dot(h, ws_ref[...], preferred_element_type=F32)
        ogt_ref[...] = _dot_nt(wgt_ref[...], h)

    o_ref[...] = jnp.dot(h_scr[...], w_ref[...], preferred_element_type=F32).astype(BF16)


def _inproj(x2, norm_g, w_main, w_small, w_gt, *, layer):
    t, d = x2.shape
    nm = w_main.shape[-1]
    bm = _pick(t, (1024, 512, 256, 128))
    bn = _pick(nm, (1024, 512, 256, 128))
    return pl.pallas_call(
        _inproj_kernel,
        grid=(t // bm, nm // bn),
        in_specs=[
            pl.BlockSpec((bm, d), lambda i, j: (i, 0)),
            pl.BlockSpec((1, d), lambda i, j: (0, 0)),
            pl.BlockSpec((None, d, bn), lambda i, j: (layer, 0, j)),
            pl.BlockSpec((None, d, 2 * LANES), lambda i, j: (layer, 0, 0)),
            pl.BlockSpec((None, 16, d), lambda i, j: (layer, 0, 0)),
        ],
        out_specs=[
            pl.BlockSpec((bm, bn), lambda i, j: (i, j)),
            pl.BlockSpec((bm, 2 * LANES), lambda i, j: (i, 0)),
            pl.BlockSpec((16, bm), lambda i, j: (0, i)),
        ],
        out_shape=[
            jax.ShapeDtypeStruct((t, nm), BF16),
            jax.ShapeDtypeStruct((t, 2 * LANES), F32),
            jax.ShapeDtypeStruct((16, t), F32),
        ],
        scratch_shapes=[pltpu.VMEM((bm, d), BF16)],
        compiler_params=_cparams("parallel", "arbitrary"),
    )(x2, norm_g, w_main, w_small, w_gt)


def _mlstm_kernel(q_ref, k_ref, v_ref, o_ref, z_ref, gc_ref, gr_ref, cw_ref, gbc_ref, gbr_ref, ng_ref,
                  out_ref, c_scr, m_scr, qk_scr, tail_scr, s_scr, qc_scr, *, n_heads, chunk):
    L = chunk
    qk_w = n_heads * DQK_A
    c_idx = pl.program_id(1)

    @pl.when(c_idx == 0)
    def _():
        c_scr[...] = jnp.zeros_like(c_scr)
        m_scr[...] = jnp.zeros_like(m_scr)
        tail_scr[...] = jnp.zeros_like(tail_scr)

    row = lax.broadcasted_iota(jnp.int32, (L, L), 0)
    col = lax.broadcasted_iota(jnp.int32, (L, L), 1)

    shifts = [(row - col == s).astype(BF16) for s in range(1, CONV_K)]
    sub8 = lax.broadcasted_iota(jnp.int32, (8, CONV_SLAB), 0)
    for slab in range(2 * qk_w // CONV_SLAB):
        cs = slice(slab * CONV_SLAB, (slab + 1) * CONV_SLAB)
        src_ref, lo = (q_ref, slab * CONV_SLAB) if slab * CONV_SLAB < qk_w else (k_ref, slab * CONV_SLAB - qk_w)
        x_b = src_ref[:, lo:lo + CONV_SLAB]
        x_f = x_b.astype(F32)
        tail = tail_scr[:, cs]
        acc = x_f * cw_ref[CONV_K - 1:CONV_K, cs]
        head_fix = jnp.zeros((8, CONV_SLAB), F32)
        for s in range(1, CONV_K):
            w_s = cw_ref[CONV_K - 1 - s:CONV_K - s, cs]
            acc = acc + jnp.dot(shifts[s - 1], x_b, preferred_element_type=F32) * w_s
            head_fix = head_fix + jnp.where(sub8 < s, pltpu.roll(tail, s, 0), 0.0) * w_s
        tail_scr[:, cs] = x_f[L - 8:L, :]
        scale = DQK_A ** -0.5 if slab * CONV_SLAB < qk_w else 1.0
        qk_scr[8:L, cs] = _silu(acc[8:L, :]) * scale
        qk_scr[0:8, cs] = _silu(acc[0:8, :] + head_fix) * scale

    gc = gc_ref[...] + gbc_ref[...]
    gr = gr_ref[...] + gbr_ref[...]
    causal = col <= row
    tri = causal.astype(F32)
    tri_t = (row <= col).astype(F32)
    b_c_all = jnp.dot(tri, _log_sigmoid(gc), preferred_element_type=F32, precision=lax.Precision.HIGHEST)
    b_r_all = jnp.dot(_log_sigmoid(gr), tri_t, preferred_element_type=F32, precision=lax.Precision.HIGHEST)

    ones_cols = jnp.ones((L, LANES), BF16)

    def head_slices(h):
        return (slice(h * DQK_A, (h + 1) * DQK_A), slice(qk_w + h * DQK_A, qk_w + (h + 1) * DQK_A),
                slice(h * DV_A, (h + 1) * DV_A))

    m_olds = []
    for h in range(n_heads):
        qs, ks, vs = head_slices(h)
        q_b = qk_scr[:, qs].astype(BF16)
        k_f = qk_scr[:, ks]
        v_ext = jnp.concatenate([v_ref[:, vs], ones_cols], axis=1)
        s_scr[h] = _dot_nt(q_b, k_f.astype(BF16))
        c_prev = c_scr[h]
        qc_scr[h] = jnp.dot(q_b, c_prev.astype(BF16), preferred_element_type=F32)

        i_c = gc[:, h:h + 1]
        b_c = b_c_all[:, n_heads + h:n_heads + h + 1]
        m_prev = m_scr[h:h + 1, 0:1]
        g_tot = b_c[L - 1:L, :]
        w_c = g_tot - b_c + i_c
        m_new = jnp.maximum(g_tot + m_prev, jnp.max(w_c, axis=0, keepdims=True))
        carry_scale = jnp.exp(g_tot + m_prev - m_new)
        kw = k_f * jnp.exp(w_c - m_new)
        c_scr[h] = carry_scale * c_prev + _dot_tn(kw.astype(BF16), v_ext)
        m_scr[h:h + 1, :] = jnp.broadcast_to(m_new, (1, LANES))
        m_olds.append(m_prev)

    for h in range(n_heads):
        qs, ks, vs = head_slices(h)
        v_ext = jnp.concatenate([v_ref[:, vs], ones_cols], axis=1)
        b_c = b_c_all[:, n_heads + h:n_heads + h + 1]
        i_r = gr[h:h + 1, :]
        b_r = b_r_all[n_heads + h:n_heads + h + 1, :]

        d_mat = jnp.where(causal, b_c + (i_r - b_r), NEG_INF)
        inter = b_c + m_olds[h]
        m_i = jnp.maximum(jnp.max(d_mat, axis=-1, keepdims=True), inter)
        p = s_scr[h] * jnp.exp(d_mat - m_i)
        decay = jnp.exp(inter - m_i)
        num_den = jnp.dot(p.astype(BF16), v_ext, preferred_element_type=F32) + decay * qc_scr[h]
        den = num_den[:, DV_A:DV_A + LANES]
        inv = 1.0 / jnp.maximum(jnp.abs(den), jnp.exp(-m_i))
        hh = num_den[:, 0:DV_A] * jnp.concatenate([inv] * (DV_A // LANES), axis=1)

        ms = jnp.mean(hh * hh, axis=-1, keepdims=True)
        hn = hh * lax.rsqrt(ms + EPS) * ng_ref[0:1, vs]
        gated = _sigmoid(o_ref[:, vs].astype(F32)) * hn * _silu(z_ref[:, vs].astype(F32))
        out_ref[:, vs] = gated.astype(BF16)


def _mlstm(proj, small, gates_t, conv_w, gb_col, gb_row, norm_g, *, batch, seq, n_heads, offs):
    t = batch * seq
    L = _pick(seq, (MLSTM_CHUNK, 128, 64))
    nc = seq // L
    qk_w = n_heads * DQK_A
    dv_w = n_heads * DV_A
    assert offs["q"] % qk_w == 0 and offs["k"] % qk_w == 0
    assert offs["v"] % dv_w == 0 and offs["o"] % dv_w == 0 and offs["za"] % dv_w == 0
    qb, kb = offs["q"] // qk_w, offs["k"] // qk_w
    vb, ob, zb = offs["v"] // dv_w, offs["o"] // dv_w, offs["za"] // dv_w
    kern = functools.partial(_mlstm_kernel, n_heads=n_heads, chunk=L)
    return pl.pallas_call(
        kern,
        grid=(batch, nc),
        in_specs=[
            pl.BlockSpec((L, qk_w), lambda b, c: (b * nc + c, qb)),
            pl.BlockSpec((L, qk_w), lambda b, c: (b * nc + c, kb)),
            pl.BlockSpec((L, dv_w), lambda b, c: (b * nc + c, vb)),
            pl.BlockSpec((L, dv_w), lambda b, c: (b * nc + c, ob)),
            pl.BlockSpec((L, dv_w), lambda b, c: (b * nc + c, zb)),
            pl.BlockSpec((L, LANES), lambda b, c: (b * nc + c, 1)),
            pl.BlockSpec((16, L), lambda b, c: (0, b * nc + c)),
            pl.BlockSpec((CONV_K, 2 * qk_w), lambda b, c: (0, 0)),
            pl.BlockSpec((1, LANES), lambda b, c: (0, 0)),
            pl.BlockSpec((16, 1), lambda b, c: (0, 0)),
            pl.BlockSpec((1, dv_w), lambda b, c: (0, 0)),
        ],
        out_specs=pl.BlockSpec((L, dv_w), lambda b, c: (b * nc + c, 0)),
        out_shape=jax.ShapeDtypeStruct((t, dv_w), BF16),
        scratch_shapes=[
            pltpu.VMEM((n_heads, DQK_A, DV_A + LANES), F32),
            pltpu.VMEM((8, LANES), F32),
            pltpu.VMEM((L, 2 * qk_w), F32),
            pltpu.VMEM((8, 2 * qk_w), F32),
            pltpu.VMEM((n_heads, L, L), F32),
            pltpu.VMEM((n_heads, L, DV_A + LANES), F32),
        ],
        compiler_params=_cparams("parallel", "arbitrary"),
    )(proj, proj, proj, proj, proj, small, gates_t, conv_w, gb_col, gb_row, norm_g)


def _mla_prep_kernel(cq_ref, ckv_ref, kr_ref, cost_ref, sint_ref, cosk_ref, sink_ref, qlg_ref, kvlg_ref,
                     wuqt_ref, wk_ref, wvt_ref, qng_ref, kng_ref, kngs_ref, qt_ref, k_ref, vt_ref, *, n_heads):
    cq = cq_ref[...].astype(F32)
    cqn = (cq * lax.rsqrt(jnp.mean(cq * cq, axis=-1, keepdims=True) + EPS) * qlg_ref[...]).astype(BF16)
    ckv = ckv_ref[...].astype(F32)
    ckvn = (ckv * lax.rsqrt(jnp.mean(ckv * ckv, axis=-1, keepdims=True) + EPS) * kvlg_ref[...]).astype(BF16)

    cos_t = cost_ref[0]
    sin_t = sint_ref[0]
    half = D_ROPE // 2
    q_t = _dot_nt(wuqt_ref[...], cqn)
    for h in range(n_heads):
        qh = q_t[h * D_HQK:(h + 1) * D_HQK, :]
        ss = jnp.sum(qh * qh, axis=0, keepdims=True)
        rs = lax.rsqrt(ss * (1.0 / D_HQK) + EPS) * (D_HQK ** -0.5 * LOG2_E)
        qn = qh * rs * qng_ref[...]
        x1 = qn[D_NOPE:D_NOPE + half, :]
        x2 = qn[D_NOPE + half:D_HQK, :]
        qt_ref[0, h, 0:D_NOPE, :] = qn[0:D_NOPE, :].astype(BF16)
        qt_ref[0, h, D_NOPE:D_NOPE + half, :] = (x1 * cos_t - x2 * sin_t).astype(BF16)
        qt_ref[0, h, D_NOPE + half:D_HQK, :] = (x2 * cos_t + x1 * sin_t).astype(BF16)

    kn = jnp.dot(ckvn, wk_ref[...], preferred_element_type=F32)
    kr = kr_ref[:, 0:D_ROPE]
    kr_sw = kr_ref[:, D_ROPE:2 * D_ROPE]
    ssr = jnp.sum(kr * kr, axis=-1, keepdims=True)
    k_rot = kr * kng_ref[:, D_NOPE:D_HQK] * cosk_ref[0] + kr_sw * kngs_ref[...] * sink_ref[0]
    for h in range(n_heads):
        kh = kn[:, h * D_NOPE:(h + 1) * D_NOPE]
        ss = jnp.sum(kh * kh, axis=-1, keepdims=True) + ssr
        rs = lax.rsqrt(ss * (1.0 / D_HQK) + EPS)
        k_ref[0, h, :, 0:D_NOPE] = (kh * rs * kng_ref[:, 0:D_NOPE]).astype(BF16)
        k_ref[0, h, :, D_NOPE:D_HQK] = (k_rot * rs).astype(BF16)

    vt_ref[0] = _dot_nt(wvt_ref[...], ckvn).astype(BF16)


def _mla_prep(proj3, small3, cos_t, sin_t, cos_k, sin_k, q_lat_g, kv_lat_g, wuq_t, wk, wv_t, qn_g, kn_g, kn_g_sw,
              *, n_heads, offs, layer):
    batch, seq, _ = proj3.shape
    ql, kvl = wuq_t.shape[-1], wk.shape[-2]
    ts = _pick(seq, (PREP_ROWS, 128))
    assert offs["cq"] % ql == 0 and offs["ckv"] % kvl == 0
    cqb, ckvb = offs["cq"] // ql, offs["ckv"] // kvl
    half = D_ROPE // 2
    kern = functools.partial(_mla_prep_kernel, n_heads=n_heads)
    const = lambda b, s: (0, 0)
    return pl.pallas_call(
        kern,
        grid=(batch, seq // ts),
        in_specs=[
            pl.BlockSpec((None, ts, ql), lambda b, s: (b, s, cqb)),
            pl.BlockSpec((None, ts, kvl), lambda b, s: (b, s, ckvb)),
            pl.BlockSpec((None, ts, LANES), lambda b, s: (b, s, 0)),
            pl.BlockSpec((1, half, ts), lambda b, s: (b, 0, s)),
            pl.BlockSpec((1, half, ts), lambda b, s: (b, 0, s)),
            pl.BlockSpec((1, ts, D_ROPE), lambda b, s: (b, s, 0)),
            pl.BlockSpec((1, ts, D_ROPE), lambda b, s: (b, s, 0)),
            pl.BlockSpec((1, ql), const),
            pl.BlockSpec((1, kvl), const),
            pl.BlockSpec((None, n_heads * D_HQK, ql), lambda b, s: (layer, 0, 0)),
            pl.BlockSpec((None, kvl, n_heads * D_NOPE), lambda b, s: (layer, 0, 0)),
            pl.BlockSpec((None, n_heads * D_VB, kvl), lambda b, s: (layer, 0, 0)),
            pl.BlockSpec((D_HQK, 1), const),
            pl.BlockSpec((1, D_HQK), const),
            pl.BlockSpec((1, D_ROPE), const),
        ],
        out_specs=[
            pl.BlockSpec((1, n_heads, D_HQK, ts), lambda b, s: (b, 0, 0, s)),
            pl.BlockSpec((1, n_heads, ts, D_HQK), lambda b, s: (b, 0, s, 0)),
            pl.BlockSpec((1, n_heads * D_VB, ts), lambda b, s: (b, 0, s)),
        ],
        out_shape=[
            jax.ShapeDtypeStruct((batch, n_heads, D_HQK, seq), BF16),
            jax.ShapeDtypeStruct((batch, n_heads, seq, D_HQK), BF16),
            jax.ShapeDtypeStruct((batch, n_heads * D_VB, seq), BF16),
        ],
        compiler_params=_cparams("parallel", "parallel"),
    )(proj3, proj3, small3, cos_t, sin_t, cos_k, sin_k, q_lat_g, kv_lat_g, wuq_t, wk, wv_t, qn_g, kn_g, kn_g_sw)


def _attn_kernel(qt_ref, k_ref, vt_ref, z_ref, o_ref, sa_scr, sb_scr, acc_scr, *, qb, kb, nsub):
    qi = pl.program_id(2)
    kpt = nsub * qb // kb
    assert kpt == 2
    base = qi * kpt
    every = tuple(range(nsub))
    ones_rows = jnp.where(lax.broadcasted_iota(jnp.int32, (ATTN_PAD_ROWS, kb), 0) == 0, 1.0, 0.0).astype(BF16)

    def scores(j, dst, chains):
        start = pl.multiple_of(j * kb, kb)
        k_blk = k_ref[0, 0, pl.ds(start, kb), :]
        for c in chains:
            q_t = qt_ref[0, 0, :, c * qb:(c + 1) * qb]
            dst[c] = jnp.dot(k_blk, q_t, preferred_element_type=F32)

    def consume(j, src, ms, chains, diag_r):
        start = pl.multiple_of(j * kb, kb)
        v_ext = jnp.concatenate([vt_ref[0, :, pl.ds(start, kb)], ones_rows], axis=0)
        ms = list(ms)
        for c in chains:
            s = src[c]
            if diag_r is not None:
                lead = c * qb - diag_r * kb
                if lead < kb - 1:
                    kk = lax.broadcasted_iota(jnp.int32, (kb, qb), 0)
                    qq = lax.broadcasted_iota(jnp.int32, (kb, qb), 1)
                    s = jnp.where(kk <= qq + lead, s, NEG_INF)
            m_new = jnp.maximum(ms[c], jnp.max(s, axis=0, keepdims=True))
            p = jnp.exp2(s - m_new).astype(BF16)
            alpha = jnp.exp2(ms[c] - m_new)
            acc_scr[c] = alpha * acc_scr[c] + jnp.dot(v_ext, p, preferred_element_type=F32)
            ms[c] = m_new
        return tuple(ms)

    acc_scr[...] = jnp.zeros_like(acc_scr)
    ms = tuple(jnp.full((1, qb), NEG_INF, F32) for _ in every)
    scores(0, sa_scr, every)

    def body(jj, ms):
        j0 = 2 * jj
        scores(j0 + 1, sb_scr, every)
        ms = consume(j0, sa_scr, ms, every, None)
        scores(j0 + 2, sa_scr, every)
        return consume(j0 + 1, sb_scr, ms, every, None)

    ms = lax.fori_loop(0, base // 2, body, ms)

    bufs = (sa_scr, sb_scr)
    seen_by = lambda r: tuple(c for c in every if (c + 1) * qb > r * kb)
    for r in range(kpt):
        if r + 1 < kpt:
            scores(base + r + 1, bufs[(r + 1) % 2], seen_by(r + 1))
        ms = consume(base + r, bufs[r % 2], ms, seen_by(r), r)

    for c in every:
        rows = slice(c * qb, (c + 1) * qb)
        o = (acc_scr[c, 0:D_VB, :] / acc_scr[c, D_VB:D_VB + 1, :]).T
        o_ref[0, rows, :] = (o * _silu(z_ref[0, rows, :].astype(F32))).astype(BF16)


def _attention(q_t, k, v_t, proj3, *, offs):
    batch, n_heads, _, seq = q_t.shape
    tq = _pick(seq, (ATTN_TILE, ATTN_TILE // 2))
    nsub = ATTN_SUBBLOCKS
    qb, kb = tq // nsub, tq // 2
    assert offs["zb"] % D_VB == 0
    zb0 = offs["zb"] // D_VB
    kern = functools.partial(_attn_kernel, qb=qb, kb=kb, nsub=nsub)
    return pl.pallas_call(
        kern,
        grid=(batch, n_heads, seq // tq),
        in_specs=[
            pl.BlockSpec((1, 1, D_HQK, tq), lambda b, h, i: (b, h, 0, i)),
            pl.BlockSpec((1, 1, seq, D_HQK), lambda b, h, i: (b, h, 0, 0)),
            pl.BlockSpec((1, D_VB, seq), lambda b, h, i: (b, h, 0)),
            pl.BlockSpec((1, tq, D_VB), lambda b, h, i: (b, i, zb0 + h)),
        ],
        out_specs=pl.BlockSpec((1, tq, D_VB), lambda b, h, i: (b, i, h)),
        out_shape=jax.ShapeDtypeStruct((batch, seq, n_heads * D_VB), BF16),
        scratch_shapes=[
            pltpu.VMEM((nsub, kb, qb), F32),
            pltpu.VMEM((nsub, kb, qb), F32),
            pltpu.VMEM((nsub, D_VB + ATTN_PAD_ROWS, qb), F32),
        ],
        compiler_params=_cparams("parallel", "parallel", "arbitrary"),
    )(q_t, k, v_t, proj3)


def _merge_kernel(ha_ref, ob_ref, wa_ref, wb_ref, ga_ref, gb_ref, y_ref):
    ya = jnp.dot(ha_ref[...], wa_ref[...], preferred_element_type=F32)
    yb = jnp.dot(ob_ref[...], wb_ref[...], preferred_element_type=F32)
    y = _sigmoid(ga_ref[...].astype(F32)) * ya + _sigmoid(gb_ref[...].astype(F32)) * yb
    y_ref[...] = y.astype(BF16)


def _merge(ha, ob, w_a, w_b, proj, *, offs, layer):
    t, da = ha.shape
    db = ob.shape[1]
    d = w_a.shape[-1]
    bm = _pick(t, (1024, 512, 256, 128))
    bn = _pick(d, (1024, 512, 256, 128))
    assert offs["ga"] % bn == 0 and offs["gb"] % bn == 0
    ga0, gb0 = offs["ga"] // bn, offs["gb"] // bn
    return pl.pallas_call(
        _merge_kernel,
        grid=(t // bm, d // bn),
        in_specs=[
            pl.BlockSpec((bm, da), lambda i, j: (i, 0)),
            pl.BlockSpec((bm, db), lambda i, j: (i, 0)),
            pl.BlockSpec((None, da, bn), lambda i, j: (layer, 0, j)),
            pl.BlockSpec((None, db, bn), lambda i, j: (layer, 0, j)),
            pl.BlockSpec((bm, bn), lambda i, j: (i, ga0 + j)),
            pl.BlockSpec((bm, bn), lambda i, j: (i, gb0 + j)),
        ],
        out_specs=pl.BlockSpec((bm, bn), lambda i, j: (i, j)),
        out_shape=jax.ShapeDtypeStruct((t, d), BF16),
        compiler_params=_cparams("parallel", "arbitrary"),
    )(ha, ob, w_a, w_b, proj, proj)


def _outproj_kernel(y_ref, w_ref, x_ref, o_ref):
    o_ref[...] = x_ref[...] + jnp.dot(y_ref[...], w_ref[...], preferred_element_type=F32)


def _outproj(y, w_out, x2, *, layer):
    t, d = y.shape
    bm = _pick(t, (1024, 512, 256, 128))
    bn = _pick(d, (1024, 512, 256, 128))
    return pl.pallas_call(
        _outproj_kernel,
        grid=(t // bm, d // bn),
        in_specs=[
            pl.BlockSpec((bm, d), lambda i, j: (i, 0)),
            pl.BlockSpec((None, d, bn), lambda i, j: (layer, 0, j)),
            pl.BlockSpec((bm, bn), lambda i, j: (i, j)),
        ],
        out_specs=pl.BlockSpec((bm, bn), lambda i, j: (i, j)),
        out_shape=jax.ShapeDtypeStruct((t, d), F32),
        compiler_params=_cparams("parallel", "arbitrary"),
    )(y, w_out, x2)


def kernel(x, positions, norm_g, w_in, gate_bias, conv_w, mlstm_norm_g, w_a, q_lat_g, kv_lat_g, w_uq, w_ukv,
           q_norm_g, k_norm_g, w_b, w_out):
    batch, seq, d = x.shape
    depth = w_in.shape[0]
    h_a = gate_bias.shape[-1] // 2
    h_b = w_uq.shape[-1] // D_HQK
    qk_w, d_a, d_b = h_a * DQK_A, h_a * DV_A, h_b * D_VB
    ql, kvl = w_uq.shape[1], w_ukv.shape[1]
    assert 2 * h_a <= 16 and w_ukv.shape[-1] == h_b * (D_NOPE + D_VB)

    sizes = (qk_w, qk_w, d_a, d_a, h_a, h_a, d_a, ql, kvl, D_ROPE, d_b, d, d)
    names = ("q", "k", "v", "o", "i", "f", "za", "cq", "ckv", "kr", "zb", "ga", "gb")
    src, acc = {}, 0
    for nme, sz in zip(names, sizes):
        src[nme] = (acc, acc + sz)
        acc += sz
    assert acc == w_in.shape[-1]

    order = ("q", "k", "v", "o", "za", "zb", "ga", "gb", "cq", "ckv")
    offs, acc = {}, 0
    for nme in order:
        offs[nme] = acc
        acc += src[nme][1] - src[nme][0]
    w_in_b = w_in.astype(BF16)
    w_main = jnp.concatenate([w_in_b[:, :, src[n][0]:src[n][1]] for n in order], axis=-1)

    half = D_ROPE // 2
    w_kr = w_in_b[:, :, src["kr"][0]:src["kr"][1]]
    w_kr_sw = jnp.concatenate([w_kr[..., half:], w_kr[..., :half]], axis=-1)
    w_if = w_in_b[:, :, src["i"][0]:src["f"][1]]
    pad_c = jnp.zeros((depth, d, LANES - 2 * h_a), BF16)
    w_small = jnp.concatenate([w_kr, w_kr_sw, w_if, pad_c], axis=-1)
    w_gt = jnp.concatenate([jnp.swapaxes(w_if, 1, 2), jnp.zeros((depth, 16 - 2 * h_a, d), BF16)], axis=1)

    gb_col = jnp.pad(gate_bias, ((0, 0), (0, LANES - 2 * h_a)))[:, None, :]
    gb_row = jnp.pad(gate_bias, ((0, 0), (0, 16 - 2 * h_a)))[:, :, None]

    w_a_b, w_b_b, w_out_b = w_a.astype(BF16), w_b.astype(BF16), w_out.astype(BF16)
    wuq_t = jnp.swapaxes(w_uq, 1, 2).astype(BF16)
    w_ukv4 = w_ukv.reshape(depth, kvl, h_b, D_NOPE + D_VB)
    wk = w_ukv4[..., :D_NOPE].reshape(depth, kvl, h_b * D_NOPE).astype(BF16)
    wv_t = jnp.swapaxes(w_ukv4[..., D_NOPE:].reshape(depth, kvl, h_b * D_VB), 1, 2).astype(BF16)
    qn_g = q_norm_g[:, :, None]
    kn_g = k_norm_g[:, None, :]
    kn_g_sw = jnp.concatenate([k_norm_g[:, D_NOPE + half:], k_norm_g[:, D_NOPE:D_NOPE + half]], axis=-1)[:, None, :]

    inv_freq = jnp.exp(-math.log(ROPE_THETA) * jnp.arange(0, D_ROPE, 2, dtype=F32) / D_ROPE)
    ang = positions.astype(F32)[..., None] * inv_freq
    cos, sin = jnp.cos(ang), jnp.sin(ang)
    cos_t, sin_t = jnp.swapaxes(cos, 1, 2), jnp.swapaxes(sin, 1, 2)
    cos_k = jnp.concatenate([cos, cos], axis=-1)
    sin_k = jnp.concatenate([-sin, sin], axis=-1)

    x2 = x.reshape(batch * seq, d)
    for l in range(depth):
        proj, small, gates_t = _inproj(x2, norm_g[l][None, :], w_main, w_small, w_gt, layer=l)
        h_gated = _mlstm(proj, small, gates_t, conv_w[l], gb_col[l], gb_row[l], mlstm_norm_g[l][None, :],
                         batch=batch, seq=seq, n_heads=h_a, offs=offs)
        proj3 = proj.reshape(batch, seq, -1)
        small3 = small.reshape(batch, seq, -1)
        q_t, k, v_t = _mla_prep(proj3, small3, cos_t, sin_t, cos_k, sin_k, q_lat_g[l][None, :], kv_lat_g[l][None, :],
                                wuq_t, wk, wv_t, qn_g[l], kn_g[l], kn_g_sw[l], n_heads=h_b, offs=offs, layer=l)
        o_gated = _attention(q_t, k, v_t, proj3, offs=offs).reshape(batch * seq, d_b)
        y = _merge(h_gated, o_gated, w_a_b, w_b_b, proj, offs=offs, layer=l)
        x2 = _outproj(y, w_out_b, x2, layer=l)
    return x2.reshape(batch, seq, d)
```

```python
import functools
import math

import jax
import jax.numpy as jnp
from jax import lax
from jax.experimental import pallas as pl
from jax.experimental.pallas import tpu as pltpu

F32 = jnp.float32
BF16 = jnp.bfloat16

DQK_A = 128
DV_A = 256
CONV_K = 4
D_NOPE = 128
D_ROPE = 64
D_HQK = D_NOPE + D_ROPE
D_VB = 128
ROPE_THETA = 10000.0
EPS = 1e-6
NEG_INF = -1e30
LOG2_E = 1.4426950408889634

LANES = 128
ROW_ALIGN = 16
VMEM_LIMIT_BYTES = 56 * 1024 * 1024

MLSTM_CHUNK = 256
CONV_SLAB = 256
ATTN_TILE = 2048
ATTN_QBLOCK = 256
ATTN_PAD_ROWS = 16
PREP_ROWS = 512
NORM_ROWS = 256


def _pick(n, candidates):
    for c in candidates:
        if n % c == 0:
            return c
    raise ValueError(f"no tile in {candidates} divides {n}")


def _cparams(*sem):
    return pltpu.CompilerParams(dimension_semantics=sem, vmem_limit_bytes=VMEM_LIMIT_BYTES)


def _sigmoid(x):
    return 1.0 / (1.0 + jnp.exp2(x * (-LOG2_E)))


def _silu(x):
    return x * _sigmoid(x)


def _log_sigmoid(x):
    return jnp.minimum(x, 0.0) - jnp.log(1.0 + jnp.exp(-jnp.abs(x)))


def _dot_nt(a, b):
    return lax.dot_general(a, b, (((1,), (1,)), ((), ())), preferred_element_type=F32)


def _dot_tn(a, b):
    return lax.dot_general(a, b, (((0,), (0,)), ((), ())), preferred_element_type=F32)


def _inproj_kernel(rows_ref, x_ref, g_ref, wt_ref, wst_ref, o_ref, os_ref, ogt_ref, h_scr):
    del rows_ref
    @pl.when(pl.program_id(1) == 0)
    def _():
        rows = x_ref.shape[0]
        step = min(NORM_ROWS, rows)
        for r0 in range(0, rows, step):
            x = x_ref[r0:r0 + step, :]
            ms = jnp.mean(x * x, axis=-1, keepdims=True)
            h_scr[r0:r0 + step, :] = (x * lax.rsqrt(ms + EPS) * g_ref[...]).astype(BF16)
        small = _dot_nt(h_scr[...], wst_ref[...])
        os_ref[...] = small
        ogt_ref[...] = small[:, LANES:2 * LANES].T[0:16, :]

    o_ref[...] = _dot_nt(h_scr[...], wt_ref[0]).astype(BF16)


def _inproj(x2, norm_g, w_t, tile_rows, w_small_t, *, layer, bn):
    t, d = x2.shape
    n_tiles = tile_rows.shape[0]
    nm = n_tiles * bn
    bm = _pick(t, (1024, 512, 256, 128))
    grid_spec = pltpu.PrefetchScalarGridSpec(
        num_scalar_prefetch=1,
        grid=(t // bm, n_tiles),
        in_specs=[
            pl.BlockSpec((bm, d), lambda i, j, rows: (i, 0)),
            pl.BlockSpec((1, d), lambda i, j, rows: (0, 0)),
            pl.BlockSpec((pl.Element(1), pl.Element(bn), pl.Element(d)),
                         lambda i, j, rows: (layer, rows[j] * ROW_ALIGN, 0)),
            pl.BlockSpec((None, 2 * LANES, d), lambda i, j, rows: (layer, 0, 0)),
        ],
        out_specs=[
            pl.BlockSpec((bm, bn), lambda i, j, rows: (i, j)),
            pl.BlockSpec((bm, 2 * LANES), lambda i, j, rows: (i, 0)),
            pl.BlockSpec((16, bm), lambda i, j, rows: (0, i)),
        ],
        scratch_shapes=[pltpu.VMEM((bm, d), BF16)],
    )
    return pl.pallas_call(
        _inproj_kernel,
        grid_spec=grid_spec,
        out_shape=[
            jax.ShapeDtypeStruct((t, nm), BF16),
            jax.ShapeDtypeStruct((t, 2 * LANES), F32),
            jax.ShapeDtypeStruct((16, t), F32),
        ],
        compiler_params=_cparams("parallel", "arbitrary"),
    )(tile_rows, x2, norm_g, w_t, w_small_t)


def _mlstm_kernel(q_ref, k_ref, v_ref, o_ref, z_ref, gc_ref, gr_ref, cw_ref, gbc_ref, gbr_ref, ng_ref,
                  out_ref, c_scr, m_scr, qk_scr, tail_scr, s_scr, qc_scr, *, n_heads, chunk):
    L = chunk
    qk_w = n_heads * DQK_A
    c_idx = pl.program_id(1)

    @pl.when(c_idx == 0)
    def _():
        c_scr[...] = jnp.zeros_like(c_scr)
        m_scr[...] = jnp.zeros_like(m_scr)
        tail_scr[...] = jnp.zeros_like(tail_scr)

    row = lax.broadcasted_iota(jnp.int32, (L, L), 0)
    col = lax.broadcasted_iota(jnp.int32, (L, L), 1)

    shifts = [(row - col == s).astype(BF16) for s in range(1, CONV_K)]
    sub8 = lax.broadcasted_iota(jnp.int32, (8, CONV_SLAB), 0)
    for slab in range(2 * qk_w // CONV_SLAB):
        cs = slice(slab * CONV_SLAB, (slab + 1) * CONV_SLAB)
        src_ref, lo = (q_ref, slab * CONV_SLAB) if slab * CONV_SLAB < qk_w else (k_ref, slab * CONV_SLAB - qk_w)
        x_b = src_ref[:, lo:lo + CONV_SLAB]
        x_f = x_b.astype(F32)
        tail = tail_scr[:, cs]
        acc = x_f * cw_ref[CONV_K - 1:CONV_K, cs]
        head_fix = jnp.zeros((8, CONV_SLAB), F32)
        for s in range(1, CONV_K):
            w_s = cw_ref[CONV_K - 1 - s:CONV_K - s, cs]
            acc = acc + jnp.dot(shifts[s - 1], x_b, preferred_element_type=F32) * w_s
            head_fix = head_fix + jnp.where(sub8 < s, pltpu.roll(tail, s, 0), 0.0) * w_s
        tail_scr[:, cs] = x_f[L - 8:L, :]
        scale = DQK_A ** -0.5 if slab * CONV_SLAB < qk_w else 1.0
        qk_scr[8:L, cs] = _silu(acc[8:L, :]) * scale
        qk_scr[0:8, cs] = _silu(acc[0:8, :] + head_fix) * scale

    gc = gc_ref[...] + gbc_ref[...]
    gr = gr_ref[...] + gbr_ref[...]
    causal = col <= row
    tri = causal.astype(F32)
    tri_t = (row <= col).astype(F32)
    b_c_all = jnp.dot(tri, _log_sigmoid(gc), preferred_element_type=F32, precision=lax.Precision.HIGHEST)
    b_r_all = jnp.dot(_log_sigmoid(gr), tri_t, preferred_element_type=F32, precision=lax.Precision.HIGHEST)

    ones_cols = jnp.ones((L, LANES), BF16)

    def head_slices(h):
        return (slice(h * DQK_A, (h + 1) * DQK_A), slice(qk_w + h * DQK_A, qk_w + (h + 1) * DQK_A),
                slice(h * DV_A, (h + 1) * DV_A))

    m_olds = []
    for h in range(n_heads):
        qs, ks, vs = head_slices(h)
        q_b = qk_scr[:, qs].astype(BF16)
        k_f = qk_scr[:, ks]
        v_ext = jnp.concatenate([v_ref[:, vs], ones_cols], axis=1)
        s_scr[h] = _dot_nt(q_b, k_f.astype(BF16))
        c_prev = c_scr[h]
        qc_scr[h] = jnp.dot(q_b, c_prev.astype(BF16), preferred_element_type=F32)

        i_c = gc[:, h:h + 1]
        b_c = b_c_all[:, n_heads + h:n_heads + h + 1]
        m_prev = m_scr[h:h + 1, 0:1]
        g_tot = b_c[L - 1:L, :]
        w_c = g_tot - b_c + i_c
        m_new = jnp.maximum(g_tot + m_prev, jnp.max(w_c, axis=0, keepdims=True))
        carry_scale = jnp.exp(g_tot + m_prev - m_new)
        kw = k_f * jnp.exp(w_c - m_new)
        c_scr[h] = carry_scale * c_prev + _dot_tn(kw.astype(BF16), v_ext)
        m_scr[h:h + 1, :] = jnp.broadcast_to(m_new, (1, LANES))
        m_olds.append(m_prev)

    for h in range(n_heads):
        qs, ks, vs = head_slices(h)
        v_ext = jnp.concatenate([v_ref[:, vs], ones_cols], axis=1)
        b_c = b_c_all[:, n_heads + h:n_heads + h + 1]
        i_r = gr[h:h + 1, :]
        b_r = b_r_all[n_heads + h:n_heads + h + 1, :]

        d_mat = jnp.where(causal, b_c + (i_r - b_r), NEG_INF)
        inter = b_c + m_olds[h]
        m_i = jnp.maximum(jnp.max(d_mat, axis=-1, keepdims=True), inter)
        p = s_scr[h] * jnp.exp(d_mat - m_i)
        decay = jnp.exp(inter - m_i)
        num_den = jnp.dot(p.astype(BF16), v_ext, preferred_element_type=F32) + decay * qc_scr[h]
        den = num_den[:, DV_A:DV_A + LANES]
        inv = 1.0 / jnp.maximum(jnp.abs(den), jnp.exp(-m_i))
        hh = num_den[:, 0:DV_A] * jnp.concatenate([inv] * (DV_A // LANES), axis=1)

        ms = jnp.mean(hh * hh, axis=-1, keepdims=True)
        hn = hh * lax.rsqrt(ms + EPS) * ng_ref[0:1, vs]
        gated = _sigmoid(o_ref[:, vs].astype(F32)) * hn * _silu(z_ref[:, vs].astype(F32))
        out_ref[:, vs] = gated.astype(BF16)


def _mlstm(proj, small, gates_t, conv_w, gb_col, gb_row, norm_g, *, batch, seq, n_heads, offs):
    t = batch * seq
    L = _pick(seq, (MLSTM_CHUNK, 128, 64))
    nc = seq // L
    qk_w = n_heads * DQK_A
    dv_w = n_heads * DV_A
    assert offs["q"] % qk_w == 0 and offs["k"] % qk_w == 0
    assert offs["v"] % dv_w == 0 and offs["o"] % dv_w == 0 and offs["za"] % dv_w == 0
    qb, kb = offs["q"] // qk_w, offs["k"] // qk_w
    vb, ob, zb = offs["v"] // dv_w, offs["o"] // dv_w, offs["za"] // dv_w
    kern = functools.partial(_mlstm_kernel, n_heads=n_heads, chunk=L)
    return pl.pallas_call(
        kern,
        grid=(batch, nc),
        in_specs=[
            pl.BlockSpec((L, qk_w), lambda b, c: (b * nc + c, qb)),
            pl.BlockSpec((L, qk_w), lambda b, c: (b * nc + c, kb)),
            pl.BlockSpec((L, dv_w), lambda b, c: (b * nc + c, vb)),
            pl.BlockSpec((L, dv_w), lambda b, c: (b * nc + c, ob)),
            pl.BlockSpec((L, dv_w), lambda b, c: (b * nc + c, zb)),
            pl.BlockSpec((L, LANES), lambda b, c: (b * nc + c, 1)),
            pl.BlockSpec((16, L), lambda b, c: (0, b * nc + c)),
            pl.BlockSpec((CONV_K, 2 * qk_w), lambda b, c: (0, 0)),
            pl.BlockSpec((1, LANES), lambda b, c: (0, 0)),
            pl.BlockSpec((16, 1), lambda b, c: (0, 0)),
            pl.BlockSpec((1, dv_w), lambda b, c: (0, 0)),
        ],
        out_specs=pl.BlockSpec((L, dv_w), lambda b, c: (b * nc + c, 0)),
        out_shape=jax.ShapeDtypeStruct((t, dv_w), BF16),
        scratch_shapes=[
            pltpu.VMEM((n_heads, DQK_A, DV_A + LANES), F32),
            pltpu.VMEM((8, LANES), F32),
            pltpu.VMEM((L, 2 * qk_w), F32),
            pltpu.VMEM((8, 2 * qk_w), F32),
            pltpu.VMEM((n_heads, L, L), F32),
            pltpu.VMEM((n_heads, L, DV_A + LANES), F32),
        ],
        compiler_params=_cparams("parallel", "arbitrary"),
    )(proj, proj, proj, proj, proj, small, gates_t, conv_w, gb_col, gb_row, norm_g)


def _mla_prep_kernel(cq_ref, ckv_ref, kr_ref, cost_ref, sint_ref, cosk_ref, sink_ref, qlg_ref, kvlg_ref,
                     wuqt_ref, wk_ref, wvt_ref, qng_ref, kng_ref, kngs_ref, qt_ref, k_ref, vt_ref, *, n_heads):
    cq = cq_ref[...].astype(F32)
    cqn = (cq * lax.rsqrt(jnp.mean(cq * cq, axis=-1, keepdims=True) + EPS) * qlg_ref[...]).astype(BF16)
    ckv = ckv_ref[...].astype(F32)
    ckvn = (ckv * lax.rsqrt(jnp.mean(ckv * ckv, axis=-1, keepdims=True) + EPS) * kvlg_ref[...]).astype(BF16)

    cos_t = cost_ref[0]
    sin_t = sint_ref[0]
    half = D_ROPE // 2
    q_t = _dot_nt(wuqt_ref[...], cqn)
    for h in range(n_heads):
        qh = q_t[h * D_HQK:(h + 1) * D_HQK, :]
        ss = jnp.sum(qh * qh, axis=0, keepdims=True)
        rs = lax.rsqrt(ss * (1.0 / D_HQK) + EPS) * (D_HQK ** -0.5 * LOG2_E)
        qn = qh * rs * qng_ref[...]
        x1 = qn[D_NOPE:D_NOPE + half, :]
        x2 = qn[D_NOPE + half:D_HQK, :]
        qt_ref[0, h, 0:D_NOPE, :] = qn[0:D_NOPE, :].astype(BF16)
        qt_ref[0, h, D_NOPE:D_NOPE + half, :] = (x1 * cos_t - x2 * sin_t).astype(BF16)
        qt_ref[0, h, D_NOPE + half:D_HQK, :] = (x2 * cos_t + x1 * sin_t).astype(BF16)

    kn = jnp.dot(ckvn, wk_ref[...], preferred_element_type=F32)
    kr = kr_ref[:, 0:D_ROPE]
    kr_sw = kr_ref[:, D_ROPE:2 * D_ROPE]
    ssr = jnp.sum(kr * kr, axis=-1, keepdims=True)
    k_rot = kr * kng_ref[:, D_NOPE:D_HQK] * cosk_ref[0] + kr_sw * kngs_ref[...] * sink_ref[0]
    for h in range(n_heads):
        kh = kn[:, h * D_NOPE:(h + 1) * D_NOPE]
        ss = jnp.sum(kh * kh, axis=-1, keepdims=True) + ssr
        rs = lax.rsqrt(ss * (1.0 / D_HQK) + EPS)
        k_ref[0, h, :, 0:D_NOPE] = (kh * rs * kng_ref[:, 0:D_NOPE]).astype(BF16)
        k_ref[0, h, :, D_NOPE:D_HQK] = (k_rot * rs).astype(BF16)

    vt_ref[0] = _dot_nt(wvt_ref[...], ckvn).astype(BF16)


def _mla_prep(proj3, small3, cos_t, sin_t, cos_k, sin_k, q_lat_g, kv_lat_g, wuq_t, wk, wv_t, qn_g, kn_g, kn_g_sw,
              *, n_heads, offs, layer):
    batch, seq, _ = proj3.shape
    ql, kvl = wuq_t.shape[-1], wk.shape[-2]
    ts = _pick(seq, (PREP_ROWS, 128))
    assert offs["cq"] % ql == 0 and offs["ckv"] % kvl == 0
    cqb, ckvb = offs["cq"] // ql, offs["ckv"] // kvl
    half = D_ROPE // 2
    kern = functools.partial(_mla_prep_kernel, n_heads=n_heads)
    const = lambda b, s: (0, 0)
    return pl.pallas_call(
        kern,
        grid=(batch, seq // ts),
        in_specs=[
            pl.BlockSpec((None, ts, ql), lambda b, s: (b, s, cqb)),
            pl.BlockSpec((None, ts, kvl), lambda b, s: (b, s, ckvb)),
            pl.BlockSpec((None, ts, LANES), lambda b, s: (b, s, 0)),
            pl.BlockSpec((1, half, ts), lambda b, s: (b, 0, s)),
            pl.BlockSpec((1, half, ts), lambda b, s: (b, 0, s)),
            pl.BlockSpec((1, ts, D_ROPE), lambda b, s: (b, s, 0)),
            pl.BlockSpec((1, ts, D_ROPE), lambda b, s: (b, s, 0)),
            pl.BlockSpec((1, ql), const),
            pl.BlockSpec((1, kvl), const),
            pl.BlockSpec((None, n_heads * D_HQK, ql), lambda b, s: (layer, 0, 0)),
            pl.BlockSpec((None, kvl, n_heads * D_NOPE), lambda b, s: (layer, 0, 0)),
            pl.BlockSpec((None, n_heads * D_VB, kvl), lambda b, s: (layer, 0, 0)),
            pl.BlockSpec((D_HQK, 1), const),
            pl.BlockSpec((1, D_HQK), const),
            pl.BlockSpec((1, D_ROPE), const),
        ],
        out_specs=[
            pl.BlockSpec((1, n_heads, D_HQK, ts), lambda b, s: (b, 0, 0, s)),
            pl.BlockSpec((1, n_heads, ts, D_HQK), lambda b, s: (b, 0, s, 0)),
            pl.BlockSpec((1, n_heads * D_VB, ts), lambda b, s: (b, 0, s)),
        ],
        out_shape=[
            jax.ShapeDtypeStruct((batch, n_heads, D_HQK, seq), BF16),
            jax.ShapeDtypeStruct((batch, n_heads, seq, D_HQK), BF16),
            jax.ShapeDtypeStruct((batch, n_heads * D_VB, seq), BF16),
        ],
        compiler_params=_cparams("parallel", "parallel"),
    )(proj3, proj3, small3, cos_t, sin_t, cos_k, sin_k, q_lat_g, kv_lat_g, wuq_t, wk, wv_t, qn_g, kn_g, kn_g_sw)


def _attn_kernel(qt_ref, k_ref, vt_ref, z_ref, o_ref, sa_scr, sb_scr, acc_scr, *, qb, kb, nsub):
    qi = pl.program_id(2)
    kpt = nsub * qb // kb
    assert kpt % 2 == 0
    base = qi * kpt
    every = tuple(range(nsub))
    ones_rows = jnp.where(lax.broadcasted_iota(jnp.int32, (ATTN_PAD_ROWS, kb), 0) == 0, 1.0, 0.0).astype(BF16)

    def scores(j, dst, chains):
        start = pl.multiple_of(j * kb, kb)
        k_blk = k_ref[0, 0, pl.ds(start, kb), :]
        for c in chains:
            q_t = qt_ref[0, 0, :, c * qb:(c + 1) * qb]
            dst[c] = jnp.dot(k_blk, q_t, preferred_element_type=F32)

    def consume(j, src, ms, chains, diag_r):
        start = pl.multiple_of(j * kb, kb)
        v_ext = jnp.concatenate([vt_ref[0, :, pl.ds(start, kb)], ones_rows], axis=0)
        ms = list(ms)
        for c in chains:
            s = src[c]
            if diag_r is not None:
                lead = c * qb - diag_r * kb
                if lead < kb - 1:
                    kk = lax.broadcasted_iota(jnp.int32, (kb, qb), 0)
                    qq = lax.broadcasted_iota(jnp.int32, (kb, qb), 1)
                    s = jnp.where(kk <= qq + lead, s, NEG_INF)
            m_new = jnp.maximum(ms[c], jnp.max(s, axis=0, keepdims=True))
            p = jnp.exp2(s - m_new).astype(BF16)
            alpha = jnp.exp2(ms[c] - m_new)
            acc_scr[c] = alpha * acc_scr[c] + jnp.dot(v_ext, p, preferred_element_type=F32)
            ms[c] = m_new
        return tuple(ms)

    acc_scr[...] = jnp.zeros_like(acc_scr)
    ms = tuple(jnp.full((1, qb), NEG_INF, F32) for _ in every)
    scores(0, sa_scr, every)

    def body(jj, ms):
        j0 = 2 * jj
        scores(j0 + 1, sb_scr, every)
        ms = consume(j0, sa_scr, ms, every, None)
        scores(j0 + 2, sa_scr, every)
        return consume(j0 + 1, sb_scr, ms, every, None)

    ms = lax.fori_loop(0, base // 2, body, ms)

    bufs = (sa_scr, sb_scr)
    seen_by = lambda r: tuple(c for c in every if (c + 1) * qb > r * kb)
    for r in range(kpt):
        if r + 1 < kpt:
            scores(base + r + 1, bufs[(r + 1) % 2], seen_by(r + 1))
        ms = consume(base + r, bufs[r % 2], ms, seen_by(r), r)

    for c in every:
        rows = slice(c * qb, (c + 1) * qb)
        o = (acc_scr[c, 0:D_VB, :] / acc_scr[c, D_VB:D_VB + 1, :]).T
        o_ref[0, rows, :] = (o * _silu(z_ref[0, rows, :].astype(F32))).astype(BF16)


def _attention(q_t, k, v_t, proj3, *, offs):
    batch, n_heads, _, seq = q_t.shape
    tq = _pick(seq, (ATTN_TILE, ATTN_TILE // 2, ATTN_TILE // 4))
    qb = min(ATTN_QBLOCK, tq // 4)
    kb, nsub = 2 * qb, tq // qb
    assert offs["zb"] % D_VB == 0
    zb0 = offs["zb"] // D_VB
    kern = functools.partial(_attn_kernel, qb=qb, kb=kb, nsub=nsub)
    return pl.pallas_call(
        kern,
        grid=(batch, n_heads, seq // tq),
        in_specs=[
            pl.BlockSpec((1, 1, D_HQK, tq), lambda b, h, i: (b, h, 0, i)),
            pl.BlockSpec((1, 1, seq, D_HQK), lambda b, h, i: (b, h, 0, 0)),
            pl.BlockSpec((1, D_VB, seq), lambda b, h, i: (b, h, 0)),
            pl.BlockSpec((1, tq, D_VB), lambda b, h, i: (b, i, zb0 + h)),
        ],
        out_specs=pl.BlockSpec((1, tq, D_VB), lambda b, h, i: (b, i, h)),
        out_shape=jax.ShapeDtypeStruct((batch, seq, n_heads * D_VB), BF16),
        scratch_shapes=[
            pltpu.VMEM((nsub, kb, qb), F32),
            pltpu.VMEM((nsub, kb, qb), F32),
            pltpu.VMEM((nsub, D_VB + ATTN_PAD_ROWS, qb), F32),
        ],
        compiler_params=_cparams("parallel", "parallel", "arbitrary"),
    )(q_t, k, v_t, proj3)


def _merge_kernel(ha_ref, ob_ref, wa_ref, wb_ref, ga_ref, gb_ref, y_ref):
    ya = jnp.dot(ha_ref[...], wa_ref[...], preferred_element_type=F32)
    yb = jnp.dot(ob_ref[...], wb_ref[...], preferred_element_type=F32)
    y = _sigmoid(ga_ref[...].astype(F32)) * ya + _sigmoid(gb_ref[...].astype(F32)) * yb
    y_ref[...] = y.astype(BF16)


def _merge(ha, ob, w_a, w_b, proj, *, offs, layer):
    t, da = ha.shape
    db = ob.shape[1]
    d = w_a.shape[-1]
    bm = _pick(t, (1024, 512, 256, 128))
    bn = _pick(d, (1024, 512, 256, 128))
    assert offs["ga"] % bn == 0 and offs["gb"] % bn == 0
    ga0, gb0 = offs["ga"] // bn, offs["gb"] // bn
    return pl.pallas_call(
        _merge_kernel,
        grid=(t // bm, d // bn),
        in_specs=[
            pl.BlockSpec((bm, da), lambda i, j: (i, 0)),
            pl.BlockSpec((bm, db), lambda i, j: (i, 0)),
            pl.BlockSpec((None, da, bn), lambda i, j: (layer, 0, j)),
            pl.BlockSpec((None, db, bn), lambda i, j: (layer, 0, j)),
            pl.BlockSpec((bm, bn), lambda i, j: (i, ga0 + j)),
            pl.BlockSpec((bm, bn), lambda i, j: (i, gb0 + j)),
        ],
        out_specs=pl.BlockSpec((bm, bn), lambda i, j: (i, j)),
        out_shape=jax.ShapeDtypeStruct((t, d), BF16),
        compiler_params=_cparams("parallel", "arbitrary"),
    )(ha, ob, w_a, w_b, proj, proj)


def _outproj_kernel(y_ref, w_ref, x_ref, o_ref):
    o_ref[...] = x_ref[...] + jnp.dot(y_ref[...], w_ref[...], preferred_element_type=F32)


def _outproj(y, w_out, x2, *, layer):
    t, d = y.shape
    bm = _pick(t, (1024, 512, 256, 128))
    bn = _pick(d, (1024, 512, 256, 128))
    return pl.pallas_call(
        _outproj_kernel,
        grid=(t // bm, d // bn),
        in_specs=[
            pl.BlockSpec((bm, d), lambda i, j: (i, 0)),
            pl.BlockSpec((None, d, bn), lambda i, j: (layer, 0, j)),
            pl.BlockSpec((bm, bn), lambda i, j: (i, j)),
        ],
        out_specs=pl.BlockSpec((bm, bn), lambda i, j: (i, j)),
        out_shape=jax.ShapeDtypeStruct((t, d), F32),
        compiler_params=_cparams("parallel", "arbitrary"),
    )(y, w_out, x2)


def kernel(x, positions, norm_g, w_in, gate_bias, conv_w, mlstm_norm_g, w_a, q_lat_g, kv_lat_g, w_uq, w_ukv,
           q_norm_g, k_norm_g, w_b, w_out):
    batch, seq, d = x.shape
    depth = w_in.shape[0]
    h_a = gate_bias.shape[-1] // 2
    h_b = w_uq.shape[-1] // D_HQK
    qk_w, d_a, d_b = h_a * DQK_A, h_a * DV_A, h_b * D_VB
    ql, kvl = w_uq.shape[1], w_ukv.shape[1]
    assert 2 * h_a <= 16 and w_ukv.shape[-1] == h_b * (D_NOPE + D_VB)

    sizes = (qk_w, qk_w, d_a, d_a, h_a, h_a, d_a, ql, kvl, D_ROPE, d_b, d, d)
    names = ("q", "k", "v", "o", "i", "f", "za", "cq", "ckv", "kr", "zb", "ga", "gb")
    src, acc = {}, 0
    for nme, sz in zip(names, sizes):
        src[nme] = (acc, acc + sz)
        acc += sz
    assert acc == w_in.shape[-1]

    order = ("q", "k", "v", "o", "za", "cq", "ckv", "zb", "ga", "gb")
    offs, acc = {}, 0
    for nme in order:
        offs[nme] = acc
        acc += src[nme][1] - src[nme][0]
    bn = _pick(acc, (1536, 1024, 512, 256, 128))
    tile_rows = []
    for first, last in (("q", "o"), ("za", "ckv"), ("zb", "gb")):
        lo, hi = src[first][0], src[last][1]
        assert lo % ROW_ALIGN == 0 and (hi - lo) % bn == 0 and bn % ROW_ALIGN == 0
        tile_rows += [r // ROW_ALIGN for r in range(lo, hi, bn)]
    tile_rows = jnp.asarray(tile_rows, jnp.int32)

    w_t = jnp.swapaxes(w_in, 1, 2).astype(BF16)
    half = D_ROPE // 2
    kr0 = src["kr"][0]
    w_small_t = jnp.concatenate([
        w_t[:, kr0:kr0 + D_ROPE], w_t[:, kr0 + half:kr0 + D_ROPE], w_t[:, kr0:kr0 + half],
        w_t[:, src["i"][0]:src["f"][1]],
        jnp.zeros((depth, LANES - 2 * h_a, d), BF16)], axis=1)

    gb_col = jnp.pad(gate_bias, ((0, 0), (0, LANES - 2 * h_a)))[:, None, :]
    gb_row = jnp.pad(gate_bias, ((0, 0), (0, 16 - 2 * h_a)))[:, :, None]

    w_a_b, w_b_b, w_out_b = w_a.astype(BF16), w_b.astype(BF16), w_out.astype(BF16)
    wuq_t = jnp.swapaxes(w_uq, 1, 2).astype(BF16)
    w_ukv4 = w_ukv.reshape(depth, kvl, h_b, D_NOPE + D_VB)
    wk = w_ukv4[..., :D_NOPE].reshape(depth, kvl, h_b * D_NOPE).astype(BF16)
    wv_t = jnp.swapaxes(w_ukv4[..., D_NOPE:].reshape(depth, kvl, h_b * D_VB), 1, 2).astype(BF16)
    qn_g = q_norm_g[:, :, None]
    kn_g = k_norm_g[:, None, :]
    kn_g_sw = jnp.concatenate([k_norm_g[:, D_NOPE + half:], k_norm_g[:, D_NOPE:D_NOPE + half]], axis=-1)[:, None, :]

    inv_freq = jnp.exp(-math.log(ROPE_THETA) * jnp.arange(0, D_ROPE, 2, dtype=F32) / D_ROPE)
    ang = positions.astype(F32)[..., None] * inv_freq
    cos, sin = jnp.cos(ang), jnp.sin(ang)
    cos_t, sin_t = jnp.swapaxes(cos, 1, 2), jnp.swapaxes(sin, 1, 2)
    cos_k = jnp.concatenate([cos, cos], axis=-1)
    sin_k = jnp.concatenate([-sin, sin], axis=-1)

    x2 = x.reshape(batch * seq, d)
    for l in range(depth):
        proj, small, gates_t = _inproj(x2, norm_g[l][None, :], w_t, tile_rows, w_small_t, layer=l, bn=bn)
        h_gated = _mlstm(proj, small, gates_t, conv_w[l], gb_col[l], gb_row[l], mlstm_norm_g[l][None, :],
                         batch=batch, seq=seq, n_heads=h_a, offs=offs)
        proj3 = proj.reshape(batch, seq, -1)
        small3 = small.reshape(batch, seq, -1)
        q_t, k, v_t = _mla_prep(proj3, small3, cos_t, sin_t, cos_k, sin_k, q_lat_g[l][None, :], kv_lat_g[l][None, :],
                                wuq_t, wk, wv_t, qn_g[l], kn_g[l], kn_g_sw[l], n_heads=h_b, offs=offs, layer=l)
        o_gated = _attention(q_t, k, v_t, proj3, offs=offs).reshape(batch * seq, d_b)
        y = _merge(h_gated, o_gated, w_a_b, w_b_b, proj, offs=offs, layer=l)
        x2 = _outproj(y, w_out_b, x2, layer=l)
    return x2.reshape(batch, seq, d)
```

```python
import functools
import math

import jax
import jax.numpy as jnp
from jax import lax
from jax.experimental import pallas as pl
from jax.experimental.pallas import tpu as pltpu

F32 = jnp.float32
BF16 = jnp.bfloat16

DQK_A = 128
DV_A = 256
CONV_K = 4
D_NOPE = 128
D_ROPE = 64
D_HQK = D_NOPE + D_ROPE
D_VB = 128
ROPE_THETA = 10000.0
EPS = 1e-6
NEG_INF = -1e30
LOG2_E = 1.4426950408889634

LANES = 128
ROW_ALIGN = 16
VMEM_LIMIT_BYTES = 56 * 1024 * 1024

MLSTM_CHUNK = 256
CONV_SLAB = 256
ATTN_TILE = 2048
ATTN_QBLOCK = 256
ATTN_PAD_ROWS = 16
PREP_ROWS = 256
NORM_ROWS = 256


def _pick(n, candidates):
    for c in candidates:
        if n % c == 0:
            return c
    raise ValueError(f"no tile in {candidates} divides {n}")


def _cparams(*sem):
    return pltpu.CompilerParams(dimension_semantics=sem, vmem_limit_bytes=VMEM_LIMIT_BYTES)


def _sigmoid(x):
    return 1.0 / (1.0 + jnp.exp2(x * (-LOG2_E)))


def _silu(x):
    return x * _sigmoid(x)


def _log_sigmoid(x):
    return jnp.minimum(x, 0.0) - jnp.log(1.0 + jnp.exp(-jnp.abs(x)))


def _prefix_max_rows(x):
    n = x.shape[0]
    rows = lax.broadcasted_iota(jnp.int32, x.shape, 0)
    k = 1
    while k < n:
        if k < 8:
            shifted = jnp.where(rows < k, NEG_INF, pltpu.roll(x, k, 0))
        else:
            shifted = jnp.concatenate([jnp.full((k, x.shape[1]), NEG_INF, x.dtype), x[:n - k, :]], axis=0)
        x = jnp.maximum(x, shifted)
        k *= 2
    return x


def _dot_nt(a, b):
    return lax.dot_general(a, b, (((1,), (1,)), ((), ())), preferred_element_type=F32)


def _dot_tn(a, b):
    return lax.dot_general(a, b, (((0,), (0,)), ((), ())), preferred_element_type=F32)


def _inproj_kernel(rows_ref, x_ref, g_ref, wt_ref, wst_ref, o_ref, os_ref, ogt_ref, h_scr):
    del rows_ref
    @pl.when(pl.program_id(1) == 0)
    def _():
        rows = x_ref.shape[0]
        step = min(NORM_ROWS, rows)
        for r0 in range(0, rows, step):
            x = x_ref[r0:r0 + step, :]
            ms = jnp.mean(x * x, axis=-1, keepdims=True)
            h_scr[r0:r0 + step, :] = (x * lax.rsqrt(ms + EPS) * g_ref[...]).astype(BF16)
        small = _dot_nt(h_scr[...], wst_ref[...])
        os_ref[...] = small
        ogt_ref[...] = small[:, LANES:2 * LANES].T[0:16, :]

    o_ref[...] = _dot_nt(h_scr[...], wt_ref[0]).astype(BF16)


def _inproj(x2, norm_g, w_t, tile_rows, w_small_t, *, layer, bn):
    t, d = x2.shape
    n_tiles = tile_rows.shape[0]
    nm = n_tiles * bn
    bm = _pick(t, (1024, 512, 256, 128))
    grid_spec = pltpu.PrefetchScalarGridSpec(
        num_scalar_prefetch=1,
        grid=(t // bm, n_tiles),
        in_specs=[
            pl.BlockSpec((bm, d), lambda i, j, rows: (i, 0)),
            pl.BlockSpec((1, d), lambda i, j, rows: (0, 0)),
            pl.BlockSpec((pl.Element(1), pl.Element(bn), pl.Element(d)),
                         lambda i, j, rows: (layer, rows[j] * ROW_ALIGN, 0)),
            pl.BlockSpec((None, 2 * LANES, d), lambda i, j, rows: (layer, 0, 0)),
        ],
        out_specs=[
            pl.BlockSpec((bm, bn), lambda i, j, rows: (i, j)),
            pl.BlockSpec((bm, 2 * LANES), lambda i, j, rows: (i, 0)),
            pl.BlockSpec((16, bm), lambda i, j, rows: (0, i)),
        ],
        scratch_shapes=[pltpu.VMEM((bm, d), BF16)],
    )
    return pl.pallas_call(
        _inproj_kernel,
        grid_spec=grid_spec,
        out_shape=[
            jax.ShapeDtypeStruct((t, nm), BF16),
            jax.ShapeDtypeStruct((t, 2 * LANES), F32),
            jax.ShapeDtypeStruct((16, t), F32),
        ],
        compiler_params=_cparams("parallel", "arbitrary"),
    )(tile_rows, x2, norm_g, w_t, w_small_t)


def _mlstm_kernel(q_ref, k_ref, v_ref, o_ref, z_ref, gc_ref, gr_ref, cw_ref, gbc_ref, gbr_ref, ng_ref,
                  out_ref, c_scr, m_scr, qk_scr, tail_scr, s_scr, cb_scr, *, n_heads, chunk):
    L = chunk
    qk_w = n_heads * DQK_A
    c_idx = pl.program_id(1)

    @pl.when(c_idx == 0)
    def _():
        c_scr[...] = jnp.zeros_like(c_scr)
        m_scr[...] = jnp.zeros_like(m_scr)
        tail_scr[...] = jnp.zeros_like(tail_scr)

    row = lax.broadcasted_iota(jnp.int32, (L, L), 0)
    col = lax.broadcasted_iota(jnp.int32, (L, L), 1)

    shifts = [(row - col == s).astype(BF16) for s in range(1, CONV_K)]
    sub8 = lax.broadcasted_iota(jnp.int32, (8, CONV_SLAB), 0)
    for slab in range(2 * qk_w // CONV_SLAB):
        cs = slice(slab * CONV_SLAB, (slab + 1) * CONV_SLAB)
        src_ref, lo = (q_ref, slab * CONV_SLAB) if slab * CONV_SLAB < qk_w else (k_ref, slab * CONV_SLAB - qk_w)
        x_b = src_ref[:, lo:lo + CONV_SLAB]
        x_f = x_b.astype(F32)
        tail = tail_scr[:, cs]
        acc = x_f * cw_ref[CONV_K - 1:CONV_K, cs]
        head_fix = jnp.zeros((8, CONV_SLAB), F32)
        for s in range(1, CONV_K):
            w_s = cw_ref[CONV_K - 1 - s:CONV_K - s, cs]
            acc = acc + jnp.dot(shifts[s - 1], x_b, preferred_element_type=F32) * w_s
            head_fix = head_fix + jnp.where(sub8 < s, pltpu.roll(tail, s, 0), 0.0) * w_s
        tail_scr[:, cs] = x_f[L - 8:L, :]
        scale = DQK_A ** -0.5 if slab * CONV_SLAB < qk_w else 1.0
        qk_scr[8:L, cs] = _silu(acc[8:L, :]) * scale
        qk_scr[0:8, cs] = _silu(acc[0:8, :] + head_fix) * scale

    ones_cols = jnp.ones((L, LANES), BF16)

    def head_slices(h):
        return (slice(h * DQK_A, (h + 1) * DQK_A), slice(qk_w + h * DQK_A, qk_w + (h + 1) * DQK_A),
                slice(h * DV_A, (h + 1) * DV_A))

    gc = gc_ref[...] + gbc_ref[...]
    gr = gr_ref[...] + gbr_ref[...]
    causal = col <= row
    tri = causal.astype(F32)
    tri_t = (row <= col).astype(F32)
    b_all = jnp.dot(tri, _log_sigmoid(gc), preferred_element_type=F32, precision=lax.Precision.HIGHEST)
    b_r_all = jnp.dot(_log_sigmoid(gr), tri_t, preferred_element_type=F32, precision=lax.Precision.HIGHEST)
    b_al = pltpu.roll(b_all, LANES - n_heads, 1)
    m_old = m_scr[0:1, :]
    mx = jnp.maximum(_prefix_max_rows(gc - b_al), m_old)
    decay_all = jnp.exp(m_old - mx)
    enm_all = jnp.exp(-(b_al + mx))
    g_tot = b_al[L - 1:L, :]
    w_all = g_tot - b_al + gc
    m_new = jnp.maximum(g_tot + m_old, jnp.max(w_all, axis=0, keepdims=True))
    carry_all = jnp.exp(g_tot + m_old - m_new)
    ew_all = jnp.exp(w_all - m_new)
    m_scr[0:1, :] = m_new

    for h in range(n_heads):
        qs, ks, vs = head_slices(h)
        k_f = qk_scr[:, ks]
        v_ext = jnp.concatenate([v_ref[:, vs], ones_cols], axis=1)
        s_scr[h] = _dot_nt(qk_scr[:, qs].astype(BF16), k_f.astype(BF16))
        c_prev = c_scr[h]
        cb_scr[h] = c_prev.astype(BF16)
        kw = k_f * ew_all[:, h:h + 1]
        c_scr[h] = carry_all[:, h:h + 1] * c_prev + _dot_tn(kw.astype(BF16), v_ext)

    for h in range(n_heads):
        qs, _, vs = head_slices(h)
        v_ext = jnp.concatenate([v_ref[:, vs], ones_cols], axis=1)
        r_row = gr[h:h + 1, :] - b_r_all[n_heads + h:n_heads + h + 1, :]
        p = s_scr[h] * jnp.exp(jnp.where(causal, r_row - mx[:, h:h + 1], NEG_INF))
        q_dec = qk_scr[:, qs] * decay_all[:, h:h + 1]
        lhs = jnp.concatenate([p.astype(BF16), q_dec.astype(BF16)], axis=1)
        rhs = jnp.concatenate([v_ext, cb_scr[h]], axis=0)
        num_den = jnp.dot(lhs, rhs, preferred_element_type=F32)
        den = num_den[:, DV_A:DV_A + LANES]
        inv = 1.0 / jnp.maximum(jnp.abs(den), enm_all[:, h:h + 1])
        hh = num_den[:, 0:DV_A] * jnp.concatenate([inv] * (DV_A // LANES), axis=1)

        ms = jnp.mean(hh * hh, axis=-1, keepdims=True)
        hn = hh * lax.rsqrt(ms + EPS) * ng_ref[0:1, vs]
        gated = _sigmoid(o_ref[:, vs].astype(F32)) * hn * _silu(z_ref[:, vs].astype(F32))
        out_ref[:, vs] = gated.astype(BF16)


def _mlstm(proj, small, gates_t, conv_w, gb_col, gb_row, norm_g, *, batch, seq, n_heads, offs):
    t = batch * seq
    L = _pick(seq, (MLSTM_CHUNK, 128, 64))
    nc = seq // L
    qk_w = n_heads * DQK_A
    dv_w = n_heads * DV_A
    assert offs["q"] % qk_w == 0 and offs["k"] % qk_w == 0
    assert offs["v"] % dv_w == 0 and offs["o"] % dv_w == 0 and offs["za"] % dv_w == 0
    qb, kb = offs["q"] // qk_w, offs["k"] // qk_w
    vb, ob, zb = offs["v"] // dv_w, offs["o"] // dv_w, offs["za"] // dv_w
    kern = functools.partial(_mlstm_kernel, n_heads=n_heads, chunk=L)
    return pl.pallas_call(
        kern,
        grid=(batch, nc),
        in_specs=[
            pl.BlockSpec((L, qk_w), lambda b, c: (b * nc + c, qb)),
            pl.BlockSpec((L, qk_w), lambda b, c: (b * nc + c, kb)),
            pl.BlockSpec((L, dv_w), lambda b, c: (b * nc + c, vb)),
            pl.BlockSpec((L, dv_w), lambda b, c: (b * nc + c, ob)),
            pl.BlockSpec((L, dv_w), lambda b, c: (b * nc + c, zb)),
            pl.BlockSpec((L, LANES), lambda b, c: (b * nc + c, 1)),
            pl.BlockSpec((16, L), lambda b, c: (0, b * nc + c)),
            pl.BlockSpec((CONV_K, 2 * qk_w), lambda b, c: (0, 0)),
            pl.BlockSpec((1, LANES), lambda b, c: (0, 0)),
            pl.BlockSpec((16, 1), lambda b, c: (0, 0)),
            pl.BlockSpec((1, dv_w), lambda b, c: (0, 0)),
        ],
        out_specs=pl.BlockSpec((L, dv_w), lambda b, c: (b * nc + c, 0)),
        out_shape=jax.ShapeDtypeStruct((t, dv_w), BF16),
        scratch_shapes=[
            pltpu.VMEM((n_heads, DQK_A, DV_A + LANES), F32),
            pltpu.VMEM((8, LANES), F32),
            pltpu.VMEM((L, 2 * qk_w), F32),
            pltpu.VMEM((8, 2 * qk_w), F32),
            pltpu.VMEM((n_heads, L, L), F32),
            pltpu.VMEM((n_heads, DQK_A, DV_A + LANES), BF16),
        ],
        compiler_params=_cparams("parallel", "arbitrary"),
    )(proj, proj, proj, proj, proj, small, gates_t, conv_w, gb_col, gb_row, norm_g)


def _mla_prep_kernel(cq_ref, ckv_ref, kr_ref, cost_ref, sint_ref, cosk_ref, sink_ref, qlg_ref, kvlg_ref,
                     wuqt_ref, wk_ref, wvt_ref, qng_ref, kng_ref, kngs_ref, qt_ref, k_ref, vt_ref, *, n_heads):
    cq = cq_ref[...].astype(F32)
    cqn = (cq * lax.rsqrt(jnp.mean(cq * cq, axis=-1, keepdims=True) + EPS) * qlg_ref[...]).astype(BF16)
    ckv = ckv_ref[...].astype(F32)
    ckvn = (ckv * lax.rsqrt(jnp.mean(ckv * ckv, axis=-1, keepdims=True) + EPS) * kvlg_ref[...]).astype(BF16)

    cos_t = cost_ref[0]
    sin_t = sint_ref[0]
    half = D_ROPE // 2
    q_t = _dot_nt(wuqt_ref[...], cqn)
    for h in range(n_heads):
        qh = q_t[h * D_HQK:(h + 1) * D_HQK, :]
        ss = jnp.sum(qh * qh, axis=0, keepdims=True)
        rs = lax.rsqrt(ss * (1.0 / D_HQK) + EPS) * (D_HQK ** -0.5 * LOG2_E)
        qn = qh * rs * qng_ref[...]
        x1 = qn[D_NOPE:D_NOPE + half, :]
        x2 = qn[D_NOPE + half:D_HQK, :]
        qt_ref[0, h, 0:D_NOPE, :] = qn[0:D_NOPE, :].astype(BF16)
        qt_ref[0, h, D_NOPE:D_NOPE + half, :] = (x1 * cos_t - x2 * sin_t).astype(BF16)
        qt_ref[0, h, D_NOPE + half:D_HQK, :] = (x2 * cos_t + x1 * sin_t).astype(BF16)

    kn = jnp.dot(ckvn, wk_ref[...], preferred_element_type=F32)
    kr = kr_ref[:, 0:D_ROPE]
    kr_sw = kr_ref[:, D_ROPE:2 * D_ROPE]
    ssr = jnp.sum(kr * kr, axis=-1, keepdims=True)
    k_rot = kr * kng_ref[:, D_NOPE:D_HQK] * cosk_ref[0] + kr_sw * kngs_ref[...] * sink_ref[0]
    for h in range(n_heads):
        kh = kn[:, h * D_NOPE:(h + 1) * D_NOPE]
        ss = jnp.sum(kh * kh, axis=-1, keepdims=True) + ssr
        rs = lax.rsqrt(ss * (1.0 / D_HQK) + EPS)
        k_ref[0, h, :, 0:D_NOPE] = (kh * rs * kng_ref[:, 0:D_NOPE]).astype(BF16)
        k_ref[0, h, :, D_NOPE:D_HQK] = (k_rot * rs).astype(BF16)

    vt_ref[0] = _dot_nt(wvt_ref[...], ckvn).astype(BF16)


def _mla_prep(proj3, small3, cos_t, sin_t, cos_k, sin_k, q_lat_g, kv_lat_g, wuq_t, wk, wv_t, qn_g, kn_g, kn_g_sw,
              *, n_heads, offs, layer):
    batch, seq, _ = proj3.shape
    ql, kvl = wuq_t.shape[-1], wk.shape[-2]
    ts = _pick(seq, (PREP_ROWS, 128))
    assert offs["cq"] % ql == 0 and offs["ckv"] % kvl == 0
    cqb, ckvb = offs["cq"] // ql, offs["ckv"] // kvl
    half = D_ROPE // 2
    kern = functools.partial(_mla_prep_kernel, n_heads=n_heads)
    const = lambda b, s: (0, 0)
    return pl.pallas_call(
        kern,
        grid=(batch, seq // ts),
        in_specs=[
            pl.BlockSpec((None, ts, ql), lambda b, s: (b, s, cqb)),
            pl.BlockSpec((None, ts, kvl), lambda b, s: (b, s, ckvb)),
            pl.BlockSpec((None, ts, LANES), lambda b, s: (b, s, 0)),
            pl.BlockSpec((1, half, ts), lambda b, s: (b, 0, s)),
            pl.BlockSpec((1, half, ts), lambda b, s: (b, 0, s)),
            pl.BlockSpec((1, ts, D_ROPE), lambda b, s: (b, s, 0)),
            pl.BlockSpec((1, ts, D_ROPE), lambda b, s: (b, s, 0)),
            pl.BlockSpec((1, ql), const),
            pl.BlockSpec((1, kvl), const),
            pl.BlockSpec((None, n_heads * D_HQK, ql), lambda b, s: (layer, 0, 0)),
            pl.BlockSpec((None, kvl, n_heads * D_NOPE), lambda b, s: (layer, 0, 0)),
            pl.BlockSpec((None, n_heads * D_VB, kvl), lambda b, s: (layer, 0, 0)),
            pl.BlockSpec((D_HQK, 1), const),
            pl.BlockSpec((1, D_HQK), const),
            pl.BlockSpec((1, D_ROPE), const),
        ],
        out_specs=[
            pl.BlockSpec((1, n_heads, D_HQK, ts), lambda b, s: (b, 0, 0, s)),
            pl.BlockSpec((1, n_heads, ts, D_HQK), lambda b, s: (b, 0, s, 0)),
            pl.BlockSpec((1, n_heads * D_VB, ts), lambda b, s: (b, 0, s)),
        ],
        out_shape=[
            jax.ShapeDtypeStruct((batch, n_heads, D_HQK, seq), BF16),
            jax.ShapeDtypeStruct((batch, n_heads, seq, D_HQK), BF16),
            jax.ShapeDtypeStruct((batch, n_heads * D_VB, seq), BF16),
        ],
        compiler_params=_cparams("parallel", "parallel"),
    )(proj3, proj3, small3, cos_t, sin_t, cos_k, sin_k, q_lat_g, kv_lat_g, wuq_t, wk, wv_t, qn_g, kn_g, kn_g_sw)


def _attn_kernel(qt_ref, k_ref, vt_ref, z_ref, o_ref, sa_scr, sb_scr, acc_scr, *, qb, kb, nsub):
    qi = pl.program_id(2)
    kpt = nsub * qb // kb
    assert kpt % 2 == 0
    base = qi * kpt
    every = tuple(range(nsub))
    ones_rows = jnp.where(lax.broadcasted_iota(jnp.int32, (ATTN_PAD_ROWS, kb), 0) == 0, 1.0, 0.0).astype(BF16)

    def scores(j, dst, chains):
        start = pl.multiple_of(j * kb, kb)
        k_blk = k_ref[0, 0, pl.ds(start, kb), :]
        for c in chains:
            q_t = qt_ref[0, 0, :, c * qb:(c + 1) * qb]
            dst[c] = jnp.dot(k_blk, q_t, preferred_element_type=F32)

    def consume(j, src, ms, chains, diag_r):
        start = pl.multiple_of(j * kb, kb)
        v_ext = jnp.concatenate([vt_ref[0, :, pl.ds(start, kb)], ones_rows], axis=0)
        ms = list(ms)
        for c in chains:
            s = src[c]
            if diag_r is not None:
                lead = c * qb - diag_r * kb
                if lead < kb - 1:
                    kk = lax.broadcasted_iota(jnp.int32, (kb, qb), 0)
                    qq = lax.broadcasted_iota(jnp.int32, (kb, qb), 1)
                    s = jnp.where(kk <= qq + lead, s, NEG_INF)
            m_new = jnp.maximum(ms[c], jnp.max(s, axis=0, keepdims=True))
            p = jnp.exp2(s - m_new).astype(BF16)
            alpha = jnp.exp2(ms[c] - m_new)
            acc_scr[c] = alpha * acc_scr[c] + jnp.dot(v_ext, p, preferred_element_type=F32)
            ms[c] = m_new
        return tuple(ms)

    acc_scr[...] = jnp.zeros_like(acc_scr)
    ms = tuple(jnp.full((1, qb), NEG_INF, F32) for _ in every)
    scores(0, sa_scr, every)

    def body(jj, ms):
        j0 = 2 * jj
        scores(j0 + 1, sb_scr, every)
        ms = consume(j0, sa_scr, ms, every, None)
        scores(j0 + 2, sa_scr, every)
        return consume(j0 + 1, sb_scr, ms, every, None)

    ms = lax.fori_loop(0, base // 2, body, ms)

    bufs = (sa_scr, sb_scr)
    seen_by = lambda r: tuple(c for c in every if (c + 1) * qb > r * kb)
    for r in range(kpt):
        if r + 1 < kpt:
            scores(base + r + 1, bufs[(r + 1) % 2], seen_by(r + 1))
        ms = consume(base + r, bufs[r % 2], ms, seen_by(r), r)

    for c in every:
        rows = slice(c * qb, (c + 1) * qb)
        o = (acc_scr[c, 0:D_VB, :] / acc_scr[c, D_VB:D_VB + 1, :]).T
        o_ref[0, rows, :] = (o * _silu(z_ref[0, rows, :].astype(F32))).astype(BF16)


def _attention(q_t, k, v_t, proj3, *, offs):
    batch, n_heads, _, seq = q_t.shape
    tq = _pick(seq, (ATTN_TILE, ATTN_TILE // 2, ATTN_TILE // 4))
    qb = min(ATTN_QBLOCK, tq // 4)
    kb, nsub = 2 * qb, tq // qb
    assert offs["zb"] % D_VB == 0
    zb0 = offs["zb"] // D_VB
    kern = functools.partial(_attn_kernel, qb=qb, kb=kb, nsub=nsub)
    return pl.pallas_call(
        kern,
        grid=(batch, n_heads, seq // tq),
        in_specs=[
            pl.BlockSpec((1, 1, D_HQK, tq), lambda b, h, i: (b, h, 0, i)),
            pl.BlockSpec((1, 1, seq, D_HQK), lambda b, h, i: (b, h, 0, 0)),
            pl.BlockSpec((1, D_VB, seq), lambda b, h, i: (b, h, 0)),
            pl.BlockSpec((1, tq, D_VB), lambda b, h, i: (b, i, zb0 + h)),
        ],
        out_specs=pl.BlockSpec((1, tq, D_VB), lambda b, h, i: (b, i, h)),
        out_shape=jax.ShapeDtypeStruct((batch, seq, n_heads * D_VB), BF16),
        scratch_shapes=[
            pltpu.VMEM((nsub, kb, qb), F32),
            pltpu.VMEM((nsub, kb, qb), F32),
            pltpu.VMEM((nsub, D_VB + ATTN_PAD_ROWS, qb), F32),
        ],
        compiler_params=_cparams("parallel", "parallel", "arbitrary"),
    )(q_t, k, v_t, proj3)


def _merge_kernel(ha_ref, ob_ref, wa_ref, wb_ref, ga_ref, gb_ref, y_ref):
    ya = jnp.dot(ha_ref[...], wa_ref[...], preferred_element_type=F32)
    yb = jnp.dot(ob_ref[...], wb_ref[...], preferred_element_type=F32)
    y = _sigmoid(ga_ref[...].astype(F32)) * ya + _sigmoid(gb_ref[...].astype(F32)) * yb
    y_ref[...] = y.astype(BF16)


def _merge(ha, ob, w_a, w_b, proj, *, offs, layer):
    t, da = ha.shape
    db = ob.shape[1]
    d = w_a.shape[-1]
    bm = _pick(t, (1024, 512, 256, 128))
    bn = _pick(d, (1024, 512, 256, 128))
    assert offs["ga"] % bn == 0 and offs["gb"] % bn == 0
    ga0, gb0 = offs["ga"] // bn, offs["gb"] // bn
    return pl.pallas_call(
        _merge_kernel,
        grid=(t // bm, d // bn),
        in_specs=[
            pl.BlockSpec((bm, da), lambda i, j: (i, 0)),
            pl.BlockSpec((bm, db), lambda i, j: (i, 0)),
            pl.BlockSpec((None, da, bn), lambda i, j: (layer, 0, j)),
            pl.BlockSpec((None, db, bn), lambda i, j: (layer, 0, j)),
            pl.BlockSpec((bm, bn), lambda i, j: (i, ga0 + j)),
            pl.BlockSpec((bm, bn), lambda i, j: (i, gb0 + j)),
        ],
        out_specs=pl.BlockSpec((bm, bn), lambda i, j: (i, j)),
        out_shape=jax.ShapeDtypeStruct((t, d), BF16),
        compiler_params=_cparams("parallel", "arbitrary"),
    )(ha, ob, w_a, w_b, proj, proj)


def _outproj_kernel(y_ref, w_ref, x_ref, o_ref):
    o_ref[...] = x_ref[...] + jnp.dot(y_ref[...], w_ref[...], preferred_element_type=F32)


def _outproj(y, w_out, x2, *, layer):
    t, d = y.shape
    bm = _pick(t, (1024, 512, 256, 128))
    bn = _pick(d, (1024, 512, 256, 128))
    return pl.pallas_call(
        _outproj_kernel,
        grid=(t // bm, d // bn),
        in_specs=[
            pl.BlockSpec((bm, d), lambda i, j: (i, 0)),
            pl.BlockSpec((None, d, bn), lambda i, j: (layer, 0, j)),
            pl.BlockSpec((bm, bn), lambda i, j: (i, j)),
        ],
        out_specs=pl.BlockSpec((bm, bn), lambda i, j: (i, j)),
        out_shape=jax.ShapeDtypeStruct((t, d), F32),
        compiler_params=_cparams("parallel", "arbitrary"),
    )(y, w_out, x2)


def kernel(x, positions, norm_g, w_in, gate_bias, conv_w, mlstm_norm_g, w_a, q_lat_g, kv_lat_g, w_uq, w_ukv,
           q_norm_g, k_norm_g, w_b, w_out):
    batch, seq, d = x.shape
    depth = w_in.shape[0]
    h_a = gate_bias.shape[-1] // 2
    h_b = w_uq.shape[-1] // D_HQK
    qk_w, d_a, d_b = h_a * DQK_A, h_a * DV_A, h_b * D_VB
    ql, kvl = w_uq.shape[1], w_ukv.shape[1]
    assert 2 * h_a <= 16 and w_ukv.shape[-1] == h_b * (D_NOPE + D_VB)

    sizes = (qk_w, qk_w, d_a, d_a, h_a, h_a, d_a, ql, kvl, D_ROPE, d_b, d, d)
    names = ("q", "k", "v", "o", "i", "f", "za", "cq", "ckv", "kr", "zb", "ga", "gb")
    src, acc = {}, 0
    for nme, sz in zip(names, sizes):
        src[nme] = (acc, acc + sz)
        acc += sz
    assert acc == w_in.shape[-1]

    order = ("q", "k", "v", "o", "za", "cq", "ckv", "zb", "ga", "gb")
    offs, acc = {}, 0
    for nme in order:
        offs[nme] = acc
        acc += src[nme][1] - src[nme][0]
    runs = [(src[first][0], src[last][1]) for first, last in (("q", "o"), ("za", "ckv"), ("zb", "gb"))]
    bn = next(c for c in (1536, 1024, 512, 256, 128) if all((hi - lo) % c == 0 for lo, hi in runs))
    tile_rows = []
    for lo, hi in runs:
        assert lo % ROW_ALIGN == 0 and bn % ROW_ALIGN == 0
        tile_rows += [r // ROW_ALIGN for r in range(lo, hi, bn)]
    tile_rows = jnp.asarray(tile_rows, jnp.int32)

    w_t = jnp.swapaxes(w_in, 1, 2).astype(BF16)
    half = D_ROPE // 2
    kr0 = src["kr"][0]
    w_small_t = jnp.concatenate([
        w_t[:, kr0:kr0 + D_ROPE], w_t[:, kr0 + half:kr0 + D_ROPE], w_t[:, kr0:kr0 + half],
        w_t[:, src["i"][0]:src["f"][1]],
        jnp.zeros((depth, LANES - 2 * h_a, d), BF16)], axis=1)

    gb_col = jnp.pad(gate_bias, ((0, 0), (0, LANES - 2 * h_a)))[:, None, :]
    gb_row = jnp.pad(gate_bias, ((0, 0), (0, 16 - 2 * h_a)))[:, :, None]

    w_a_b, w_b_b, w_out_b = w_a.astype(BF16), w_b.astype(BF16), w_out.astype(BF16)
    wuq_t = jnp.swapaxes(w_uq, 1, 2).astype(BF16)
    w_ukv4 = w_ukv.reshape(depth, kvl, h_b, D_NOPE + D_VB)
    wk = w_ukv4[..., :D_NOPE].reshape(depth, kvl, h_b * D_NOPE).astype(BF16)
    wv_t = jnp.swapaxes(w_ukv4[..., D_NOPE:].reshape(depth, kvl, h_b * D_VB), 1, 2).astype(BF16)
    qn_g = q_norm_g[:, :, None]
    kn_g = k_norm_g[:, None, :]
    kn_g_sw = jnp.concatenate([k_norm_g[:, D_NOPE + half:], k_norm_g[:, D_NOPE:D_NOPE + half]], axis=-1)[:, None, :]

    inv_freq = jnp.exp(-math.log(ROPE_THETA) * jnp.arange(0, D_ROPE, 2, dtype=F32) / D_ROPE)
    ang = positions.astype(F32)[..., None] * inv_freq
    cos, sin = jnp.cos(ang), jnp.sin(ang)
    cos_t, sin_t = jnp.swapaxes(cos, 1, 2), jnp.swapaxes(sin, 1, 2)
    cos_k = jnp.concatenate([cos, cos], axis=-1)
    sin_k = jnp.concatenate([-sin, sin], axis=-1)

    x2 = x.reshape(batch * seq, d)
    for l in range(depth):
        proj, small, gates_t = _inproj(x2, norm_g[l][None, :], w_t, tile_rows, w_small_t, layer=l, bn=bn)
        h_gated = _mlstm(proj, small, gates_t, conv_w[l], gb_col[l], gb_row[l], mlstm_norm_g[l][None, :],
                         batch=batch, seq=seq, n_heads=h_a, offs=offs)
        proj3 = proj.reshape(batch, seq, -1)
        small3 = small.reshape(batch, seq, -1)
        q_t, k, v_t = _mla_prep(proj3, small3, cos_t, sin_t, cos_k, sin_k, q_lat_g[l][None, :], kv_lat_g[l][None, :],
                                wuq_t, wk, wv_t, qn_g[l], kn_g[l], kn_g_sw[l], n_heads=h_b, offs=offs, layer=l)
        o_gated = _attention(q_t, k, v_t, proj3, offs=offs).reshape(batch * seq, d_b)
        y = _merge(h_gated, o_gated, w_a_b, w_b_b, proj, offs=offs, layer=l)
        x2 = _outproj(y, w_out_b, x2, layer=l)
    return x2.reshape(batch, seq, d)
```

```python
import functools
import math

import jax
import jax.numpy as jnp
from jax import lax
from jax.experimental import pallas as pl
from jax.experimental.pallas import tpu as pltpu

F32 = jnp.float32
BF16 = jnp.bfloat16

DQK_A = 128
DV_A = 256
CONV_K = 4
D_NOPE = 128
D_ROPE = 64
D_HQK = D_NOPE + D_ROPE
D_VB = 128
ROPE_THETA = 10000.0
EPS = 1e-6
NEG_INF = -1e30
LOG2_E = 1.4426950408889634

LANES = 128
ROW_ALIGN = 16
VMEM_LIMIT_BYTES = 56 * 1024 * 1024

MLSTM_CHUNK = 256
CONV_SLAB = 256
ATTN_TILE = 2048
ATTN_QBLOCK = 256
ATTN_PAD_ROWS = 16
PREP_ROWS = 256
NORM_ROWS = 256


def _pick(n, candidates):
    for c in candidates:
        if n % c == 0:
            return c
    raise ValueError(f"no tile in {candidates} divides {n}")


def _cparams(*sem):
    return pltpu.CompilerParams(dimension_semantics=sem, vmem_limit_bytes=VMEM_LIMIT_BYTES)


def _sigmoid(x):
    return 1.0 / (1.0 + jnp.exp2(x * (-LOG2_E)))


def _silu(x):
    return x * _sigmoid(x)


def _log_sigmoid(x):
    return jnp.minimum(x, 0.0) - jnp.log(1.0 + jnp.exp(-jnp.abs(x)))


def _prefix_max_rows(x):
    n = x.shape[0]
    rows = lax.broadcasted_iota(jnp.int32, x.shape, 0)
    k = 1
    while k < n:
        if k < 8:
            shifted = jnp.where(rows < k, NEG_INF, pltpu.roll(x, k, 0))
        else:
            shifted = jnp.concatenate([jnp.full((k, x.shape[1]), NEG_INF, x.dtype), x[:n - k, :]], axis=0)
        x = jnp.maximum(x, shifted)
        k *= 2
    return x


def _dot_nt(a, b):
    return lax.dot_general(a, b, (((1,), (1,)), ((), ())), preferred_element_type=F32)


def _dot_tn(a, b):
    return lax.dot_general(a, b, (((0,), (0,)), ((), ())), preferred_element_type=F32)


def _inproj_kernel(rows_ref, x_ref, g_ref, wt_ref, wst_ref, o_ref, os_ref, ogt_ref, h_scr):
    del rows_ref
    @pl.when(pl.program_id(1) == 0)
    def _():
        rows = x_ref.shape[0]
        step = min(NORM_ROWS, rows)
        for r0 in range(0, rows, step):
            x = x_ref[r0:r0 + step, :]
            ms = jnp.mean(x * x, axis=-1, keepdims=True)
            h_scr[r0:r0 + step, :] = (x * lax.rsqrt(ms + EPS) * g_ref[...]).astype(BF16)
        small = _dot_nt(h_scr[...], wst_ref[...])
        os_ref[...] = small
        ogt_ref[...] = small[:, LANES:2 * LANES].T[0:16, :]

    o_ref[...] = _dot_nt(h_scr[...], wt_ref[0]).astype(BF16)


def _inproj(x2, norm_g, w_t, tile_rows, w_small_t, *, layer, bn):
    t, d = x2.shape
    n_tiles = tile_rows.shape[0]
    nm = n_tiles * bn
    bm = _pick(t, (1024, 512, 256, 128))
    grid_spec = pltpu.PrefetchScalarGridSpec(
        num_scalar_prefetch=1,
        grid=(t // bm, n_tiles),
        in_specs=[
            pl.BlockSpec((bm, d), lambda i, j, rows: (i, 0)),
            pl.BlockSpec((1, d), lambda i, j, rows: (0, 0)),
            pl.BlockSpec((pl.Element(1), pl.Element(bn), pl.Element(d)),
                         lambda i, j, rows: (layer, rows[j] * ROW_ALIGN, 0)),
            pl.BlockSpec((None, 2 * LANES, d), lambda i, j, rows: (layer, 0, 0)),
        ],
        out_specs=[
            pl.BlockSpec((bm, bn), lambda i, j, rows: (i, j)),
            pl.BlockSpec((bm, 2 * LANES), lambda i, j, rows: (i, 0)),
            pl.BlockSpec((16, bm), lambda i, j, rows: (0, i)),
        ],
        scratch_shapes=[pltpu.VMEM((bm, d), BF16)],
    )
    return pl.pallas_call(
        _inproj_kernel,
        grid_spec=grid_spec,
        out_shape=[
            jax.ShapeDtypeStruct((t, nm), BF16),
            jax.ShapeDtypeStruct((t, 2 * LANES), F32),
            jax.ShapeDtypeStruct((16, t), F32),
        ],
        compiler_params=_cparams("parallel", "arbitrary"),
    )(tile_rows, x2, norm_g, w_t, w_small_t)


def _mlstm_kernel(q_ref, k_ref, v_ref, o_ref, z_ref, gc_ref, gr_ref, cw_ref, gbc_ref, gbr_ref, ng_ref,
                  out_ref, c_scr, m_scr, qk_scr, tail_scr, s_scr, cb_scr, *, n_heads, chunk):
    L = chunk
    qk_w = n_heads * DQK_A
    c_idx = pl.program_id(1)

    @pl.when(c_idx == 0)
    def _():
        c_scr[...] = jnp.zeros_like(c_scr)
        m_scr[...] = jnp.zeros_like(m_scr)
        tail_scr[...] = jnp.zeros_like(tail_scr)

    row = lax.broadcasted_iota(jnp.int32, (L, L), 0)
    col = lax.broadcasted_iota(jnp.int32, (L, L), 1)

    shifts = [(row - col == s).astype(BF16) for s in range(1, CONV_K)]
    sub8 = lax.broadcasted_iota(jnp.int32, (8, CONV_SLAB), 0)
    for slab in range(2 * qk_w // CONV_SLAB):
        cs = slice(slab * CONV_SLAB, (slab + 1) * CONV_SLAB)
        src_ref, lo = (q_ref, slab * CONV_SLAB) if slab * CONV_SLAB < qk_w else (k_ref, slab * CONV_SLAB - qk_w)
        x_b = src_ref[:, lo:lo + CONV_SLAB]
        x_f = x_b.astype(F32)
        tail = tail_scr[:, cs]
        acc = x_f * cw_ref[CONV_K - 1:CONV_K, cs]
        head_fix = jnp.zeros((8, CONV_SLAB), F32)
        for s in range(1, CONV_K):
            w_s = cw_ref[CONV_K - 1 - s:CONV_K - s, cs]
            acc = acc + jnp.dot(shifts[s - 1], x_b, preferred_element_type=F32) * w_s
            head_fix = head_fix + jnp.where(sub8 < s, pltpu.roll(tail, s, 0), 0.0) * w_s
        tail_scr[:, cs] = x_f[L - 8:L, :]
        scale = DQK_A ** -0.5 if slab * CONV_SLAB < qk_w else 1.0
        qk_scr[8:L, cs] = _silu(acc[8:L, :]) * scale
        qk_scr[0:8, cs] = _silu(acc[0:8, :] + head_fix) * scale

    ones_cols = jnp.ones((L, LANES), BF16)

    def head_slices(h):
        return (slice(h * DQK_A, (h + 1) * DQK_A), slice(qk_w + h * DQK_A, qk_w + (h + 1) * DQK_A),
                slice(h * DV_A, (h + 1) * DV_A))

    gc = gc_ref[...] + gbc_ref[...]
    gr = gr_ref[...] + gbr_ref[...]
    causal = col <= row
    tri = causal.astype(F32)
    tri_t = (row <= col).astype(F32)
    b_all = jnp.dot(tri, _log_sigmoid(gc), preferred_element_type=F32, precision=lax.Precision.HIGHEST)
    b_r_all = jnp.dot(_log_sigmoid(gr), tri_t, preferred_element_type=F32, precision=lax.Precision.HIGHEST)
    b_al = pltpu.roll(b_all, LANES - n_heads, 1)
    m_old = m_scr[0:1, :]
    mx = jnp.maximum(_prefix_max_rows(gc - b_al), m_old)
    decay_all = jnp.exp(m_old - mx)
    enm_all = jnp.exp(-(b_al + mx))
    g_tot = b_al[L - 1:L, :]
    w_all = g_tot - b_al + gc
    m_new = jnp.maximum(g_tot + m_old, jnp.max(w_all, axis=0, keepdims=True))
    carry_all = jnp.exp(g_tot + m_old - m_new)
    ew_all = jnp.exp(w_all - m_new)
    m_scr[0:1, :] = m_new

    for h in range(n_heads):
        qs, ks, vs = head_slices(h)
        k_f = qk_scr[:, ks]
        v_ext = jnp.concatenate([v_ref[:, vs], ones_cols], axis=1)
        s_scr[h] = _dot_nt(qk_scr[:, qs].astype(BF16), k_f.astype(BF16))
        c_prev = c_scr[h]
        cb_scr[h] = c_prev.astype(BF16)
        kw = k_f * ew_all[:, h:h + 1]
        c_scr[h] = carry_all[:, h:h + 1] * c_prev + _dot_tn(kw.astype(BF16), v_ext)

    for h in range(n_heads):
        qs, _, vs = head_slices(h)
        v_ext = jnp.concatenate([v_ref[:, vs], ones_cols], axis=1)
        r_row = gr[h:h + 1, :] - b_r_all[n_heads + h:n_heads + h + 1, :]
        p = s_scr[h] * jnp.exp(jnp.where(causal, r_row - mx[:, h:h + 1], NEG_INF))
        q_dec = qk_scr[:, qs] * decay_all[:, h:h + 1]
        lhs = jnp.concatenate([p.astype(BF16), q_dec.astype(BF16)], axis=1)
        rhs = jnp.concatenate([v_ext, cb_scr[h]], axis=0)
        num_den = jnp.dot(lhs, rhs, preferred_element_type=F32)
        den = num_den[:, DV_A:DV_A + LANES]
        inv = 1.0 / jnp.maximum(jnp.abs(den), enm_all[:, h:h + 1])
        hh = num_den[:, 0:DV_A] * jnp.concatenate([inv] * (DV_A // LANES), axis=1)

        ms = jnp.mean(hh * hh, axis=-1, keepdims=True)
        hn = hh * lax.rsqrt(ms + EPS) * ng_ref[0:1, vs]
        gated = _sigmoid(o_ref[:, vs].astype(F32)) * hn * _silu(z_ref[:, vs].astype(F32))
        out_ref[:, vs] = gated.astype(BF16)


def _mlstm(proj, small, gates_t, conv_w, gb_col, gb_row, norm_g, *, batch, seq, n_heads, offs):
    t = batch * seq
    L = _pick(seq, (MLSTM_CHUNK, 128, 64))
    nc = seq // L
    qk_w = n_heads * DQK_A
    dv_w = n_heads * DV_A
    assert offs["q"] % qk_w == 0 and offs["k"] % qk_w == 0
    assert offs["v"] % dv_w == 0 and offs["o"] % dv_w == 0 and offs["za"] % dv_w == 0
    qb, kb = offs["q"] // qk_w, offs["k"] // qk_w
    vb, ob, zb = offs["v"] // dv_w, offs["o"] // dv_w, offs["za"] // dv_w
    kern = functools.partial(_mlstm_kernel, n_heads=n_heads, chunk=L)
    return pl.pallas_call(
        kern,
        grid=(batch, nc),
        in_specs=[
            pl.BlockSpec((L, qk_w), lambda b, c: (b * nc + c, qb)),
            pl.BlockSpec((L, qk_w), lambda b, c: (b * nc + c, kb)),
            pl.BlockSpec((L, dv_w), lambda b, c: (b * nc + c, vb)),
            pl.BlockSpec((L, dv_w), lambda b, c: (b * nc + c, ob)),
            pl.BlockSpec((L, dv_w), lambda b, c: (b * nc + c, zb)),
            pl.BlockSpec((L, LANES), lambda b, c: (b * nc + c, 1)),
            pl.BlockSpec((16, L), lambda b, c: (0, b * nc + c)),
            pl.BlockSpec((CONV_K, 2 * qk_w), lambda b, c: (0, 0)),
            pl.BlockSpec((1, LANES), lambda b, c: (0, 0)),
            pl.BlockSpec((16, 1), lambda b, c: (0, 0)),
            pl.BlockSpec((1, dv_w), lambda b, c: (0, 0)),
        ],
        out_specs=pl.BlockSpec((L, dv_w), lambda b, c: (b * nc + c, 0)),
        out_shape=jax.ShapeDtypeStruct((t, dv_w), BF16),
        scratch_shapes=[
            pltpu.VMEM((n_heads, DQK_A, DV_A + LANES), F32),
            pltpu.VMEM((8, LANES), F32),
            pltpu.VMEM((L, 2 * qk_w), F32),
            pltpu.VMEM((8, 2 * qk_w), F32),
            pltpu.VMEM((n_heads, L, L), F32),
            pltpu.VMEM((n_heads, DQK_A, DV_A + LANES), BF16),
        ],
        compiler_params=_cparams("parallel", "arbitrary"),
    )(proj, proj, proj, proj, proj, small, gates_t, conv_w, gb_col, gb_row, norm_g)


def _mla_prep_kernel(cq_ref, ckv_ref, kr_ref, cost_ref, sint_ref, cosk_ref, sink_ref, qlg_ref, kvlg_ref,
                     wuqt_ref, wk_ref, wvt_ref, qng_ref, kng_ref, kngs_ref, qt_ref, k_ref, vt_ref, *, n_heads):
    cq = cq_ref[...].astype(F32)
    cqn = (cq * lax.rsqrt(jnp.mean(cq * cq, axis=-1, keepdims=True) + EPS) * qlg_ref[...]).astype(BF16)
    ckv = ckv_ref[...].astype(F32)
    ckvn = (ckv * lax.rsqrt(jnp.mean(ckv * ckv, axis=-1, keepdims=True) + EPS) * kvlg_ref[...]).astype(BF16)

    cos_t = cost_ref[0]
    sin_t = sint_ref[0]
    half = D_ROPE // 2
    q_t = _dot_nt(wuqt_ref[...], cqn)
    for h in range(n_heads):
        qh = q_t[h * D_HQK:(h + 1) * D_HQK, :]
        ss = jnp.sum(qh * qh, axis=0, keepdims=True)
        rs = lax.rsqrt(ss * (1.0 / D_HQK) + EPS) * (D_HQK ** -0.5 * LOG2_E)
        qn = qh * rs * qng_ref[...]
        x1 = qn[D_NOPE:D_NOPE + half, :]
        x2 = qn[D_NOPE + half:D_HQK, :]
        qt_ref[0, h, 0:D_NOPE, :] = qn[0:D_NOPE, :].astype(BF16)
        qt_ref[0, h, D_NOPE:D_NOPE + half, :] = (x1 * cos_t - x2 * sin_t).astype(BF16)
        qt_ref[0, h, D_NOPE + half:D_HQK, :] = (x2 * cos_t + x1 * sin_t).astype(BF16)

    kn = jnp.dot(ckvn, wk_ref[...], preferred_element_type=F32)
    kr = kr_ref[:, 0:D_ROPE]
    kr_sw = kr_ref[:, D_ROPE:2 * D_ROPE]
    ssr = jnp.sum(kr * kr, axis=-1, keepdims=True)
    k_rot = kr * kng_ref[:, D_NOPE:D_HQK] * cosk_ref[0] + kr_sw * kngs_ref[...] * sink_ref[0]
    for h in range(n_heads):
        kh = kn[:, h * D_NOPE:(h + 1) * D_NOPE]
        ss = jnp.sum(kh * kh, axis=-1, keepdims=True) + ssr
        rs = lax.rsqrt(ss * (1.0 / D_HQK) + EPS)
        k_ref[0, h, :, 0:D_NOPE] = (kh * rs * kng_ref[:, 0:D_NOPE]).astype(BF16)
        k_ref[0, h, :, D_NOPE:D_HQK] = (k_rot * rs).astype(BF16)

    vt_ref[0] = _dot_nt(wvt_ref[...], ckvn).astype(BF16)


def _mla_prep(proj3, small3, cos_t, sin_t, cos_k, sin_k, q_lat_g, kv_lat_g, wuq_t, wk, wv_t, qn_g, kn_g, kn_g_sw,
              *, n_heads, offs, layer):
    batch, seq, _ = proj3.shape
    ql, kvl = wuq_t.shape[-1], wk.shape[-2]
    ts = _pick(seq, (PREP_ROWS, 128))
    assert offs["cq"] % ql == 0 and offs["ckv"] % kvl == 0
    cqb, ckvb = offs["cq"] // ql, offs["ckv"] // kvl
    half = D_ROPE // 2
    kern = functools.partial(_mla_prep_kernel, n_heads=n_heads)
    const = lambda b, s: (0, 0)
    return pl.pallas_call(
        kern,
        grid=(batch, seq // ts),
        in_specs=[
            pl.BlockSpec((None, ts, ql), lambda b, s: (b, s, cqb)),
            pl.BlockSpec((None, ts, kvl), lambda b, s: (b, s, ckvb)),
            pl.BlockSpec((None, ts, LANES), lambda b, s: (b, s, 0)),
            pl.BlockSpec((1, half, ts), lambda b, s: (b, 0, s)),
            pl.BlockSpec((1, half, ts), lambda b, s: (b, 0, s)),
            pl.BlockSpec((1, ts, D_ROPE), lambda b, s: (b, s, 0)),
            pl.BlockSpec((1, ts, D_ROPE), lambda b, s: (b, s, 0)),
            pl.BlockSpec((1, ql), const),
            pl.BlockSpec((1, kvl), const),
            pl.BlockSpec((None, n_heads * D_HQK, ql), lambda b, s: (layer, 0, 0)),
            pl.BlockSpec((None, kvl, n_heads * D_NOPE), lambda b, s: (layer, 0, 0)),
            pl.BlockSpec((None, n_heads * D_VB, kvl), lambda b, s: (layer, 0, 0)),
            pl.BlockSpec((D_HQK, 1), const),
            pl.BlockSpec((1, D_HQK), const),
            pl.BlockSpec((1, D_ROPE), const),
        ],
        out_specs=[
            pl.BlockSpec((1, n_heads, D_HQK, ts), lambda b, s: (b, 0, 0, s)),
            pl.BlockSpec((1, n_heads, ts, D_HQK), lambda b, s: (b, 0, s, 0)),
            pl.BlockSpec((1, n_heads * D_VB, ts), lambda b, s: (b, 0, s)),
        ],
        out_shape=[
            jax.ShapeDtypeStruct((batch, n_heads, D_HQK, seq), BF16),
            jax.ShapeDtypeStruct((batch, n_heads, seq, D_HQK), BF16),
            jax.ShapeDtypeStruct((batch, n_heads * D_VB, seq), BF16),
        ],
        compiler_params=_cparams("parallel", "parallel"),
    )(proj3, proj3, small3, cos_t, sin_t, cos_k, sin_k, q_lat_g, kv_lat_g, wuq_t, wk, wv_t, qn_g, kn_g, kn_g_sw)


def _attn_kernel(qt_ref, k_ref, vt_ref, z_ref, o_ref, sa_scr, sb_scr, acc_scr, *, qb, kb, nsub):
    qi = pl.program_id(2)
    kpt = nsub * qb // kb
    assert kpt % 2 == 0
    base = qi * kpt
    every = tuple(range(nsub))
    ones_rows = {n: jnp.where(lax.broadcasted_iota(jnp.int32, (ATTN_PAD_ROWS, n), 0) == 0, 1.0, 0.0).astype(BF16)
                 for n in (qb, kb)}

    def visible(c, diag_r):
        if diag_r is None:
            return kb
        lead = c * qb - diag_r * kb
        return min(max(lead + qb, 0), kb)

    def scores(j, dst, c, n_keys):
        start = pl.multiple_of(j * kb, kb)
        k_blk = k_ref[0, 0, pl.ds(start, n_keys), :]
        q_t = qt_ref[0, 0, :, c * qb:(c + 1) * qb]
        dst[c, 0:n_keys, :] = jnp.dot(k_blk, q_t, preferred_element_type=F32)

    def consume(j, src, ms, c, diag_r):
        n_keys = visible(c, diag_r)
        start = pl.multiple_of(j * kb, kb)
        v_ext = jnp.concatenate([vt_ref[0, :, pl.ds(start, n_keys)], ones_rows[n_keys]], axis=0)
        s = src[c, 0:n_keys, :]
        if diag_r is not None and c * qb - diag_r * kb < n_keys - 1:
            kk = lax.broadcasted_iota(jnp.int32, (n_keys, qb), 0)
            qq = lax.broadcasted_iota(jnp.int32, (n_keys, qb), 1)
            s = jnp.where(kk <= qq + (c * qb - diag_r * kb), s, NEG_INF)
        m_new = jnp.maximum(ms[c], jnp.max(s, axis=0, keepdims=True))
        p = jnp.exp2(s - m_new).astype(BF16)
        alpha = jnp.exp2(ms[c] - m_new)
        acc_scr[c] = alpha * acc_scr[c] + jnp.dot(v_ext, p, preferred_element_type=F32)
        return ms[:c] + (m_new,) + ms[c + 1:]

    def phase(j_cur, src, dst, ms, diag_r, next_r):
        for c in every:
            if next_r != -1 and visible(c, next_r) > 0:
                scores(j_cur + 1, dst, c, visible(c, next_r))
            if visible(c, diag_r) > 0:
                ms = consume(j_cur, src, ms, c, diag_r)
        return ms

    acc_scr[...] = jnp.zeros_like(acc_scr)
    ms = tuple(jnp.full((1, qb), NEG_INF, F32) for _ in every)

    for c in every:
        scores(0, sa_scr, c, kb)

    def body(jj, ms):
        j0 = 2 * jj
        ms = phase(j0, sa_scr, sb_scr, ms, None, None)
        return phase(j0 + 1, sb_scr, sa_scr, ms, None, None)

    ms = lax.fori_loop(0, base // 2, body, ms)

    bufs = (sa_scr, sb_scr)
    for r in range(kpt):
        ms = phase(base + r, bufs[r % 2], bufs[(r + 1) % 2], ms, r, r + 1 if r + 1 < kpt else -1)

    for c in every:
        rows = slice(c * qb, (c + 1) * qb)
        o = (acc_scr[c, 0:D_VB, :] / acc_scr[c, D_VB:D_VB + 1, :]).T
        o_ref[0, rows, :] = (o * _silu(z_ref[0, rows, :].astype(F32))).astype(BF16)


def _attention(q_t, k, v_t, proj3, *, offs):
    batch, n_heads, _, seq = q_t.shape
    tq = _pick(seq, (ATTN_TILE, ATTN_TILE // 2, ATTN_TILE // 4))
    qb = min(ATTN_QBLOCK, tq // 4)
    kb, nsub = 2 * qb, tq // qb
    assert offs["zb"] % D_VB == 0
    zb0 = offs["zb"] // D_VB
    kern = functools.partial(_attn_kernel, qb=qb, kb=kb, nsub=nsub)
    return pl.pallas_call(
        kern,
        grid=(batch, n_heads, seq // tq),
        in_specs=[
            pl.BlockSpec((1, 1, D_HQK, tq), lambda b, h, i: (b, h, 0, i)),
            pl.BlockSpec((1, 1, seq, D_HQK), lambda b, h, i: (b, h, 0, 0)),
            pl.BlockSpec((1, D_VB, seq), lambda b, h, i: (b, h, 0)),
            pl.BlockSpec((1, tq, D_VB), lambda b, h, i: (b, i, zb0 + h)),
        ],
        out_specs=pl.BlockSpec((1, tq, D_VB), lambda b, h, i: (b, i, h)),
        out_shape=jax.ShapeDtypeStruct((batch, seq, n_heads * D_VB), BF16),
        scratch_shapes=[
            pltpu.VMEM((nsub, kb, qb), F32),
            pltpu.VMEM((nsub, kb, qb), F32),
            pltpu.VMEM((nsub, D_VB + ATTN_PAD_ROWS, qb), F32),
        ],
        compiler_params=_cparams("parallel", "parallel", "arbitrary"),
    )(q_t, k, v_t, proj3)


def _merge_kernel(ha_ref, ob_ref, wa_ref, wb_ref, ga_ref, gb_ref, y_ref):
    ya = jnp.dot(ha_ref[...], wa_ref[...], preferred_element_type=F32)
    yb = jnp.dot(ob_ref[...], wb_ref[...], preferred_element_type=F32)
    y = _sigmoid(ga_ref[...].astype(F32)) * ya + _sigmoid(gb_ref[...].astype(F32)) * yb
    y_ref[...] = y.astype(BF16)


def _merge(ha, ob, w_a, w_b, proj, *, offs, layer):
    t, da = ha.shape
    db = ob.shape[1]
    d = w_a.shape[-1]
    bm = _pick(t, (1024, 512, 256, 128))
    bn = _pick(d, (1024, 512, 256, 128))
    assert offs["ga"] % bn == 0 and offs["gb"] % bn == 0
    ga0, gb0 = offs["ga"] // bn, offs["gb"] // bn
    return pl.pallas_call(
        _merge_kernel,
        grid=(t // bm, d // bn),
        in_specs=[
            pl.BlockSpec((bm, da), lambda i, j: (i, 0)),
            pl.BlockSpec((bm, db), lambda i, j: (i, 0)),
            pl.BlockSpec((None, da, bn), lambda i, j: (layer, 0, j)),
            pl.BlockSpec((None, db, bn), lambda i, j: (layer, 0, j)),
            pl.BlockSpec((bm, bn), lambda i, j: (i, ga0 + j)),
            pl.BlockSpec((bm, bn), lambda i, j: (i, gb0 + j)),
        ],
        out_specs=pl.BlockSpec((bm, bn), lambda i, j: (i, j)),
        out_shape=jax.ShapeDtypeStruct((t, d), BF16),
        compiler_params=_cparams("parallel", "arbitrary"),
    )(ha, ob, w_a, w_b, proj, proj)


def _outproj_kernel(y_ref, w_ref, x_ref, o_ref):
    o_ref[...] = x_ref[...] + jnp.dot(y_ref[...], w_ref[...], preferred_element_type=F32)


def _outproj(y, w_out, x2, *, layer):
    t, d = y.shape
    bm = _pick(t, (512, 256, 128))
    bn = _pick(d, (2048, 1024, 512, 256, 128))
    return pl.pallas_call(
        _outproj_kernel,
        grid=(t // bm, d // bn),
        in_specs=[
            pl.BlockSpec((bm, d), lambda i, j: (i, 0)),
            pl.BlockSpec((None, d, bn), lambda i, j: (layer, 0, j)),
            pl.BlockSpec((bm, bn), lambda i, j: (i, j)),
        ],
        out_specs=pl.BlockSpec((bm, bn), lambda i, j: (i, j)),
        out_shape=jax.ShapeDtypeStruct((t, d), F32),
        compiler_params=_cparams("parallel", "arbitrary"),
    )(y, w_out, x2)


def kernel(x, positions, norm_g, w_in, gate_bias, conv_w, mlstm_norm_g, w_a, q_lat_g, kv_lat_g, w_uq, w_ukv,
           q_norm_g, k_norm_g, w_b, w_out):
    batch, seq, d = x.shape
    depth = w_in.shape[0]
    h_a = gate_bias.shape[-1] // 2
    h_b = w_uq.shape[-1] // D_HQK
    qk_w, d_a, d_b = h_a * DQK_A, h_a * DV_A, h_b * D_VB
    ql, kvl = w_uq.shape[1], w_ukv.shape[1]
    assert 2 * h_a <= 16 and w_ukv.shape[-1] == h_b * (D_NOPE + D_VB)

    sizes = (qk_w, qk_w, d_a, d_a, h_a, h_a, d_a, ql, kvl, D_ROPE, d_b, d, d)
    names = ("q", "k", "v", "o", "i", "f", "za", "cq", "ckv", "kr", "zb", "ga", "gb")
    src, acc = {}, 0
    for nme, sz in zip(names, sizes):
        src[nme] = (acc, acc + sz)
        acc += sz
    assert acc == w_in.shape[-1]

    order = ("q", "k", "v", "o", "za", "cq", "ckv", "zb", "ga", "gb")
    offs, acc = {}, 0
    for nme in order:
        offs[nme] = acc
        acc += src[nme][1] - src[nme][0]
    runs = [(src[first][0], src[last][1]) for first, last in (("q", "o"), ("za", "ckv"), ("zb", "gb"))]
    bn = next(c for c in (1536, 1024, 512, 256, 128) if all((hi - lo) % c == 0 for lo, hi in runs))
    tile_rows = []
    for lo, hi in runs:
        assert lo % ROW_ALIGN == 0 and bn % ROW_ALIGN == 0
        tile_rows += [r // ROW_ALIGN for r in range(lo, hi, bn)]
    tile_rows = jnp.asarray(tile_rows, jnp.int32)

    w_t = jnp.swapaxes(w_in, 1, 2).astype(BF16)
    half = D_ROPE // 2
    kr0 = src["kr"][0]
    w_small_t = jnp.concatenate([
        w_t[:, kr0:kr0 + D_ROPE], w_t[:, kr0 + half:kr0 + D_ROPE], w_t[:, kr0:kr0 + half],
        w_t[:, src["i"][0]:src["f"][1]],
        jnp.zeros((depth, LANES - 2 * h_a, d), BF16)], axis=1)

    gb_col = jnp.pad(gate_bias, ((0, 0), (0, LANES - 2 * h_a)))[:, None, :]
    gb_row = jnp.pad(gate_bias, ((0, 0), (0, 16 - 2 * h_a)))[:, :, None]

    w_a_b, w_b_b, w_out_b = w_a.astype(BF16), w_b.astype(BF16), w_out.astype(BF16)
    wuq_t = jnp.swapaxes(w_uq, 1, 2).astype(BF16)
    w_ukv4 = w_ukv.reshape(depth, kvl, h_b, D_NOPE + D_VB)
    wk = w_ukv4[..., :D_NOPE].reshape(depth, kvl, h_b * D_NOPE).astype(BF16)
    wv_t = jnp.swapaxes(w_ukv4[..., D_NOPE:].reshape(depth, kvl, h_b * D_VB), 1, 2).astype(BF16)
    qn_g = q_norm_g[:, :, None]
    kn_g = k_norm_g[:, None, :]
    kn_g_sw = jnp.concatenate([k_norm_g[:, D_NOPE + half:], k_norm_g[:, D_NOPE:D_NOPE + half]], axis=-1)[:, None, :]

    inv_freq = jnp.exp(-math.log(ROPE_THETA) * jnp.arange(0, D_ROPE, 2, dtype=F32) / D_ROPE)
    ang = positions.astype(F32)[..., None] * inv_freq
    cos, sin = jnp.cos(ang), jnp.sin(ang)
    cos_t, sin_t = jnp.swapaxes(cos, 1, 2), jnp.swapaxes(sin, 1, 2)
    cos_k = jnp.concatenate([cos, cos], axis=-1)
    sin_k = jnp.concatenate([-sin, sin], axis=-1)

    x2 = x.reshape(batch * seq, d)
    for l in range(depth):
        proj, small, gates_t = _inproj(x2, norm_g[l][None, :], w_t, tile_rows, w_small_t, layer=l, bn=bn)
        h_gated = _mlstm(proj, small, gates_t, conv_w[l], gb_col[l], gb_row[l], mlstm_norm_g[l][None, :],
                         batch=batch, seq=seq, n_heads=h_a, offs=offs)
        proj3 = proj.reshape(batch, seq, -1)
        small3 = small.reshape(batch, seq, -1)
        q_t, k, v_t = _mla_prep(proj3, small3, cos_t, sin_t, cos_k, sin_k, q_lat_g[l][None, :], kv_lat_g[l][None, :],
                                wuq_t, wk, wv_t, qn_g[l], kn_g[l], kn_g_sw[l], n_heads=h_b, offs=offs, layer=l)
        o_gated = _attention(q_t, k, v_t, proj3, offs=offs).reshape(batch * seq, d_b)
        y = _merge(h_gated, o_gated, w_a_b, w_b_b, proj, offs=offs, layer=l)
        x2 = _outproj(y, w_out_b, x2, layer=l)
    return x2.reshape(batch, seq, d)
```

```python
import functools
import math

import jax
import jax.numpy as jnp
from jax import lax
from jax.experimental import pallas as pl
from jax.experimental.pallas import tpu as pltpu

F32 = jnp.float32
BF16 = jnp.bfloat16

DQK_A = 128
DV_A = 256
CONV_K = 4
D_NOPE = 128
D_ROPE = 64
D_HQK = D_NOPE + D_ROPE
D_VB = 128
ROPE_THETA = 10000.0
EPS = 1e-6
NEG_INF = -1e30
LOG2_E = 1.4426950408889634

LANES = 128
ROW_ALIGN = 16
VMEM_LIMIT_BYTES = 56 * 1024 * 1024

MLSTM_CHUNK = 256
CONV_SLAB = 256
ATTN_TILE = 4096
ATTN_QBLOCK = 256
ATTN_PAD_ROWS = 16
PREP_ROWS = 256
NORM_ROWS = 256


def _pick(n, candidates):
    for c in candidates:
        if n % c == 0:
            return c
    raise ValueError(f"no tile in {candidates} divides {n}")


def _cparams(*sem):
    return pltpu.CompilerParams(dimension_semantics=sem, vmem_limit_bytes=VMEM_LIMIT_BYTES)


def _sigmoid(x):
    return 1.0 / (1.0 + jnp.exp2(x * (-LOG2_E)))


def _silu(x):
    return x * _sigmoid(x)


def _log_sigmoid(x):
    return jnp.minimum(x, 0.0) - jnp.log(1.0 + jnp.exp(-jnp.abs(x)))


def _prefix_max_rows(x):
    n = x.shape[0]
    rows = lax.broadcasted_iota(jnp.int32, x.shape, 0)
    k = 1
    while k < n:
        if k < 8:
            shifted = jnp.where(rows < k, NEG_INF, pltpu.roll(x, k, 0))
        else:
            shifted = jnp.concatenate([jnp.full((k, x.shape[1]), NEG_INF, x.dtype), x[:n - k, :]], axis=0)
        x = jnp.maximum(x, shifted)
        k *= 2
    return x


def _dot_nt(a, b):
    return lax.dot_general(a, b, (((1,), (1,)), ((), ())), preferred_element_type=F32)


def _dot_tn(a, b):
    return lax.dot_general(a, b, (((0,), (0,)), ((), ())), preferred_element_type=F32)


def _inproj_kernel(rows_ref, x_ref, g_ref, wt_ref, wst_ref, o_ref, os_ref, ogt_ref, h_scr):
    del rows_ref
    @pl.when(pl.program_id(1) == 0)
    def _():
        rows = x_ref.shape[0]
        step = min(NORM_ROWS, rows)
        for r0 in range(0, rows, step):
            x = x_ref[r0:r0 + step, :]
            ms = jnp.mean(x * x, axis=-1, keepdims=True)
            h_scr[r0:r0 + step, :] = (x * lax.rsqrt(ms + EPS) * g_ref[...]).astype(BF16)
        small = _dot_nt(h_scr[...], wst_ref[...])
        os_ref[...] = small
        ogt_ref[...] = small[:, LANES:2 * LANES].T[0:16, :]

    o_ref[...] = _dot_nt(h_scr[...], wt_ref[0]).astype(BF16)


def _inproj(x2, norm_g, w_t, tile_rows, w_small_t, *, layer, bn):
    t, d = x2.shape
    n_tiles = tile_rows.shape[0]
    nm = n_tiles * bn
    bm = _pick(t, (1024, 512, 256, 128))
    grid_spec = pltpu.PrefetchScalarGridSpec(
        num_scalar_prefetch=1,
        grid=(t // bm, n_tiles),
        in_specs=[
            pl.BlockSpec((bm, d), lambda i, j, rows: (i, 0)),
            pl.BlockSpec((1, d), lambda i, j, rows: (0, 0)),
            pl.BlockSpec((pl.Element(1), pl.Element(bn), pl.Element(d)),
                         lambda i, j, rows: (layer, rows[j] * ROW_ALIGN, 0)),
            pl.BlockSpec((None, 2 * LANES, d), lambda i, j, rows: (layer, 0, 0)),
        ],
        out_specs=[
            pl.BlockSpec((bm, bn), lambda i, j, rows: (i, j)),
            pl.BlockSpec((bm, 2 * LANES), lambda i, j, rows: (i, 0)),
            pl.BlockSpec((16, bm), lambda i, j, rows: (0, i)),
        ],
        scratch_shapes=[pltpu.VMEM((bm, d), BF16)],
    )
    return pl.pallas_call(
        _inproj_kernel,
        grid_spec=grid_spec,
        out_shape=[
            jax.ShapeDtypeStruct((t, nm), BF16),
            jax.ShapeDtypeStruct((t, 2 * LANES), F32),
            jax.ShapeDtypeStruct((16, t), F32),
        ],
        compiler_params=_cparams("parallel", "arbitrary"),
    )(tile_rows, x2, norm_g, w_t, w_small_t)


def _mlstm_kernel(q_ref, k_ref, v_ref, o_ref, z_ref, gc_ref, gr_ref, cw_ref, gbc_ref, gbr_ref, ng_ref,
                  out_ref, c_scr, m_scr, qk_scr, tail_scr, s_scr, cb_scr, *, n_heads, chunk):
    L = chunk
    qk_w = n_heads * DQK_A
    c_idx = pl.program_id(1)

    @pl.when(c_idx == 0)
    def _():
        c_scr[...] = jnp.zeros_like(c_scr)
        m_scr[...] = jnp.zeros_like(m_scr)
        tail_scr[...] = jnp.zeros_like(tail_scr)

    row = lax.broadcasted_iota(jnp.int32, (L, L), 0)
    col = lax.broadcasted_iota(jnp.int32, (L, L), 1)

    shifts = [(row - col == s).astype(BF16) for s in range(1, CONV_K)]
    sub8 = lax.broadcasted_iota(jnp.int32, (8, CONV_SLAB), 0)

    def conv_slab(slab):
        cs = slice(slab * CONV_SLAB, (slab + 1) * CONV_SLAB)
        src_ref, lo = (q_ref, slab * CONV_SLAB) if slab * CONV_SLAB < qk_w else (k_ref, slab * CONV_SLAB - qk_w)
        x_b = src_ref[:, lo:lo + CONV_SLAB]
        x_f = x_b.astype(F32)
        tail = tail_scr[:, cs]
        acc = x_f * cw_ref[CONV_K - 1:CONV_K, cs]
        head_fix = jnp.zeros((8, CONV_SLAB), F32)
        for s in range(1, CONV_K):
            w_s = cw_ref[CONV_K - 1 - s:CONV_K - s, cs]
            acc = acc + jnp.dot(shifts[s - 1], x_b, preferred_element_type=F32) * w_s
            head_fix = head_fix + jnp.where(sub8 < s, pltpu.roll(tail, s, 0), 0.0) * w_s
        tail_scr[:, cs] = x_f[L - 8:L, :]
        scale = DQK_A ** -0.5 if slab * CONV_SLAB < qk_w else 1.0
        qk_scr[8:L, cs] = _silu(acc[8:L, :]) * scale
        qk_scr[0:8, cs] = _silu(acc[0:8, :] + head_fix) * scale

    ones_cols = jnp.ones((L, LANES), BF16)

    def head_slices(h):
        return (slice(h * DQK_A, (h + 1) * DQK_A), slice(qk_w + h * DQK_A, qk_w + (h + 1) * DQK_A),
                slice(h * DV_A, (h + 1) * DV_A))

    gc = gc_ref[...] + gbc_ref[...]
    gr = gr_ref[...] + gbr_ref[...]
    causal = col <= row
    tri = causal.astype(F32)
    tri_t = (row <= col).astype(F32)
    b_all = jnp.dot(tri, _log_sigmoid(gc), preferred_element_type=F32, precision=lax.Precision.HIGHEST)
    b_r_all = jnp.dot(_log_sigmoid(gr), tri_t, preferred_element_type=F32, precision=lax.Precision.HIGHEST)
    b_al = pltpu.roll(b_all, LANES - n_heads, 1)
    m_old = m_scr[0:1, :]
    mx = jnp.maximum(_prefix_max_rows(gc - b_al), m_old)
    decay_all = jnp.exp(m_old - mx)
    enm_all = jnp.exp(-(b_al + mx))
    g_tot = b_al[L - 1:L, :]
    w_all = g_tot - b_al + gc
    m_new = jnp.maximum(g_tot + m_old, jnp.max(w_all, axis=0, keepdims=True))
    carry_all = jnp.exp(g_tot + m_old - m_new)
    ew_all = jnp.exp(w_all - m_new)
    m_scr[0:1, :] = m_new

    def stage1(h):
        qs, ks, vs = head_slices(h)
        k_f = qk_scr[:, ks]
        v_ext = jnp.concatenate([v_ref[:, vs], ones_cols], axis=1)
        s_scr[h] = _dot_nt(qk_scr[:, qs].astype(BF16), k_f.astype(BF16))
        c_prev = c_scr[h]
        cb_scr[h] = c_prev.astype(BF16)
        kw = k_f * ew_all[:, h:h + 1]
        c_scr[h] = carry_all[:, h:h + 1] * c_prev + _dot_tn(kw.astype(BF16), v_ext)

    def stage2(h):
        qs, _, vs = head_slices(h)
        v_ext = jnp.concatenate([v_ref[:, vs], ones_cols], axis=1)
        r_row = gr[h:h + 1, :] - b_r_all[n_heads + h:n_heads + h + 1, :]
        p = s_scr[h] * jnp.exp(jnp.where(causal, r_row - mx[:, h:h + 1], NEG_INF))
        q_dec = qk_scr[:, qs] * decay_all[:, h:h + 1]
        lhs = jnp.concatenate([p.astype(BF16), q_dec.astype(BF16)], axis=1)
        rhs = jnp.concatenate([v_ext, cb_scr[h]], axis=0)
        num_den = jnp.dot(lhs, rhs, preferred_element_type=F32)
        den = num_den[:, DV_A:DV_A + LANES]
        inv = 1.0 / jnp.maximum(jnp.abs(den), enm_all[:, h:h + 1])
        hh = num_den[:, 0:DV_A] * jnp.concatenate([inv] * (DV_A // LANES), axis=1)

        ms = jnp.mean(hh * hh, axis=-1, keepdims=True)
        hn = hh * lax.rsqrt(ms + EPS) * ng_ref[0:1, vs]
        gated = _sigmoid(o_ref[:, vs].astype(F32)) * hn * _silu(z_ref[:, vs].astype(F32))
        out_ref[:, vs] = gated.astype(BF16)

    for slab in range(2 * qk_w // CONV_SLAB):
        conv_slab(slab)
    for h in range(n_heads):
        stage1(h)
    for h in range(n_heads):
        stage2(h)


def _mlstm(proj, small, gates_t, conv_w, gb_col, gb_row, norm_g, *, batch, seq, n_heads, offs):
    t = batch * seq
    L = _pick(seq, (MLSTM_CHUNK, 128, 64))
    nc = seq // L
    qk_w = n_heads * DQK_A
    dv_w = n_heads * DV_A
    assert offs["q"] % qk_w == 0 and offs["k"] % qk_w == 0
    assert offs["v"] % dv_w == 0 and offs["o"] % dv_w == 0 and offs["za"] % dv_w == 0
    qb, kb = offs["q"] // qk_w, offs["k"] // qk_w
    vb, ob, zb = offs["v"] // dv_w, offs["o"] // dv_w, offs["za"] // dv_w
    kern = functools.partial(_mlstm_kernel, n_heads=n_heads, chunk=L)
    return pl.pallas_call(
        kern,
        grid=(batch, nc),
        in_specs=[
            pl.BlockSpec((L, qk_w), lambda b, c: (b * nc + c, qb)),
            pl.BlockSpec((L, qk_w), lambda b, c: (b * nc + c, kb)),
            pl.BlockSpec((L, dv_w), lambda b, c: (b * nc + c, vb)),
            pl.BlockSpec((L, dv_w), lambda b, c: (b * nc + c, ob)),
            pl.BlockSpec((L, dv_w), lambda b, c: (b * nc + c, zb)),
            pl.BlockSpec((L, LANES), lambda b, c: (b * nc + c, 1)),
            pl.BlockSpec((16, L), lambda b, c: (0, b * nc + c)),
            pl.BlockSpec((CONV_K, 2 * qk_w), lambda b, c: (0, 0)),
            pl.BlockSpec((1, LANES), lambda b, c: (0, 0)),
            pl.BlockSpec((16, 1), lambda b, c: (0, 0)),
            pl.BlockSpec((1, dv_w), lambda b, c: (0, 0)),
        ],
        out_specs=pl.BlockSpec((L, dv_w), lambda b, c: (b * nc + c, 0)),
        out_shape=jax.ShapeDtypeStruct((t, dv_w), BF16),
        scratch_shapes=[
            pltpu.VMEM((n_heads, DQK_A, DV_A + LANES), F32),
            pltpu.VMEM((8, LANES), F32),
            pltpu.VMEM((L, 2 * qk_w), F32),
            pltpu.VMEM((8, 2 * qk_w), F32),
            pltpu.VMEM((n_heads, L, L), F32),
            pltpu.VMEM((n_heads, DQK_A, DV_A + LANES), BF16),
        ],
        compiler_params=_cparams("parallel", "arbitrary"),
    )(proj, proj, proj, proj, proj, small, gates_t, conv_w, gb_col, gb_row, norm_g)


def _mla_prep_kernel(cq_ref, ckv_ref, kr_ref, cost_ref, sint_ref, cosk_ref, sink_ref, qlg_ref, kvlg_ref,
                     wuqt_ref, wk_ref, wvt_ref, qng_ref, kng_ref, kngs_ref, qt_ref, k_ref, vt_ref, *, n_heads):
    cq = cq_ref[...].astype(F32)
    cqn = (cq * lax.rsqrt(jnp.mean(cq * cq, axis=-1, keepdims=True) + EPS) * qlg_ref[...]).astype(BF16)
    ckv = ckv_ref[...].astype(F32)
    ckvn = (ckv * lax.rsqrt(jnp.mean(ckv * ckv, axis=-1, keepdims=True) + EPS) * kvlg_ref[...]).astype(BF16)

    cos_t = cost_ref[0]
    sin_t = sint_ref[0]
    half = D_ROPE // 2
    q_t = _dot_nt(wuqt_ref[...], cqn)
    for h in range(n_heads):
        qh = q_t[h * D_HQK:(h + 1) * D_HQK, :]
        ss = jnp.sum(qh * qh, axis=0, keepdims=True)
        rs = lax.rsqrt(ss * (1.0 / D_HQK) + EPS) * (D_HQK ** -0.5 * LOG2_E)
        qn = qh * rs * qng_ref[...]
        x1 = qn[D_NOPE:D_NOPE + half, :]
        x2 = qn[D_NOPE + half:D_HQK, :]
        qt_ref[0, h, 0:D_NOPE, :] = qn[0:D_NOPE, :].astype(BF16)
        qt_ref[0, h, D_NOPE:D_NOPE + half, :] = (x1 * cos_t - x2 * sin_t).astype(BF16)
        qt_ref[0, h, D_NOPE + half:D_HQK, :] = (x2 * cos_t + x1 * sin_t).astype(BF16)

    kn = jnp.dot(ckvn, wk_ref[...], preferred_element_type=F32)
    kr = kr_ref[:, 0:D_ROPE]
    kr_sw = kr_ref[:, D_ROPE:2 * D_ROPE]
    ssr = jnp.sum(kr * kr, axis=-1, keepdims=True)
    k_rot = kr * kng_ref[:, D_NOPE:D_HQK] * cosk_ref[0] + kr_sw * kngs_ref[...] * sink_ref[0]
    for h in range(n_heads):
        kh = kn[:, h * D_NOPE:(h + 1) * D_NOPE]
        ss = jnp.sum(kh * kh, axis=-1, keepdims=True) + ssr
        rs = lax.rsqrt(ss * (1.0 / D_HQK) + EPS)
        k_ref[0, h, :, 0:D_NOPE] = (kh * rs * kng_ref[:, 0:D_NOPE]).astype(BF16)
        k_ref[0, h, :, D_NOPE:D_HQK] = (k_rot * rs).astype(BF16)

    vt_ref[0] = _dot_nt(wvt_ref[...], ckvn).astype(BF16)


def _mla_prep(proj3, small3, cos_t, sin_t, cos_k, sin_k, q_lat_g, kv_lat_g, wuq_t, wk, wv_t, qn_g, kn_g, kn_g_sw,
              *, n_heads, offs, layer):
    batch, seq, _ = proj3.shape
    ql, kvl = wuq_t.shape[-1], wk.shape[-2]
    ts = _pick(seq, (PREP_ROWS, 128))
    assert offs["cq"] % ql == 0 and offs["ckv"] % kvl == 0
    cqb, ckvb = offs["cq"] // ql, offs["ckv"] // kvl
    half = D_ROPE // 2
    kern = functools.partial(_mla_prep_kernel, n_heads=n_heads)
    const = lambda b, s: (0, 0)
    return pl.pallas_call(
        kern,
        grid=(batch, seq // ts),
        in_specs=[
            pl.BlockSpec((None, ts, ql), lambda b, s: (b, s, cqb)),
            pl.BlockSpec((None, ts, kvl), lambda b, s: (b, s, ckvb)),
            pl.BlockSpec((None, ts, LANES), lambda b, s: (b, s, 0)),
            pl.BlockSpec((1, half, ts), lambda b, s: (b, 0, s)),
            pl.BlockSpec((1, half, ts), lambda b, s: (b, 0, s)),
            pl.BlockSpec((1, ts, D_ROPE), lambda b, s: (b, s, 0)),
            pl.BlockSpec((1, ts, D_ROPE), lambda b, s: (b, s, 0)),
            pl.BlockSpec((1, ql), const),
            pl.BlockSpec((1, kvl), const),
            pl.BlockSpec((None, n_heads * D_HQK, ql), lambda b, s: (layer, 0, 0)),
            pl.BlockSpec((None, kvl, n_heads * D_NOPE), lambda b, s: (layer, 0, 0)),
            pl.BlockSpec((None, n_heads * D_VB, kvl), lambda b, s: (layer, 0, 0)),
            pl.BlockSpec((D_HQK, 1), const),
            pl.BlockSpec((1, D_HQK), const),
            pl.BlockSpec((1, D_ROPE), const),
        ],
        out_specs=[
            pl.BlockSpec((1, n_heads, D_HQK, ts), lambda b, s: (b, 0, 0, s)),
            pl.BlockSpec((1, n_heads, ts, D_HQK), lambda b, s: (b, 0, s, 0)),
            pl.BlockSpec((1, n_heads * D_VB, ts), lambda b, s: (b, 0, s)),
        ],
        out_shape=[
            jax.ShapeDtypeStruct((batch, n_heads, D_HQK, seq), BF16),
            jax.ShapeDtypeStruct((batch, n_heads, seq, D_HQK), BF16),
            jax.ShapeDtypeStruct((batch, n_heads * D_VB, seq), BF16),
        ],
        compiler_params=_cparams("parallel", "parallel"),
    )(proj3, proj3, small3, cos_t, sin_t, cos_k, sin_k, q_lat_g, kv_lat_g, wuq_t, wk, wv_t, qn_g, kn_g, kn_g_sw)


def _attn_kernel(qt_ref, k_ref, vt_ref, z_ref, o_ref, sa_scr, sb_scr, acc_scr, *, qb, kb, nsub):
    qi = pl.program_id(2)
    kpt = nsub * qb // kb
    assert kpt % 2 == 0
    base = qi * kpt
    every = tuple(range(nsub))
    ones_rows = {n: jnp.where(lax.broadcasted_iota(jnp.int32, (ATTN_PAD_ROWS, n), 0) == 0, 1.0, 0.0).astype(BF16)
                 for n in (qb, kb)}

    def visible(c, diag_r):
        if diag_r is None:
            return kb
        lead = c * qb - diag_r * kb
        return min(max(lead + qb, 0), kb)

    def scores(j, dst, c, n_keys):
        start = pl.multiple_of(j * kb, kb)
        k_blk = k_ref[0, 0, pl.ds(start, n_keys), :]
        q_t = qt_ref[0, 0, :, c * qb:(c + 1) * qb]
        dst[c, 0:n_keys, :] = jnp.dot(k_blk, q_t, preferred_element_type=F32)

    def consume(j, src, ms, c, diag_r):
        n_keys = visible(c, diag_r)
        start = pl.multiple_of(j * kb, kb)
        v_ext = jnp.concatenate([vt_ref[0, :, pl.ds(start, n_keys)], ones_rows[n_keys]], axis=0)
        s = src[c, 0:n_keys, :]
        if diag_r is not None and c * qb - diag_r * kb < n_keys - 1:
            kk = lax.broadcasted_iota(jnp.int32, (n_keys, qb), 0)
            qq = lax.broadcasted_iota(jnp.int32, (n_keys, qb), 1)
            s = jnp.where(kk <= qq + (c * qb - diag_r * kb), s, NEG_INF)
        m_new = jnp.maximum(ms[c], jnp.max(s, axis=0, keepdims=True))
        p = jnp.exp2(s - m_new).astype(BF16)
        alpha = jnp.exp2(ms[c] - m_new)
        acc_scr[c] = alpha * acc_scr[c] + jnp.dot(v_ext, p, preferred_element_type=F32)
        return ms[:c] + (m_new,) + ms[c + 1:]

    def phase(j_cur, src, dst, ms, diag_r, next_r):
        for c in every:
            if next_r != -1 and visible(c, next_r) > 0:
                scores(j_cur + 1, dst, c, visible(c, next_r))
            if visible(c, diag_r) > 0:
                ms = consume(j_cur, src, ms, c, diag_r)
        return ms

    acc_scr[...] = jnp.zeros_like(acc_scr)
    ms = tuple(jnp.full((1, qb), NEG_INF, F32) for _ in every)

    for c in every:
        scores(0, sa_scr, c, kb)

    def body(jj, ms):
        j0 = 2 * jj
        ms = phase(j0, sa_scr, sb_scr, ms, None, None)
        return phase(j0 + 1, sb_scr, sa_scr, ms, None, None)

    ms = lax.fori_loop(0, base // 2, body, ms)

    bufs = (sa_scr, sb_scr)
    for r in range(kpt):
        ms = phase(base + r, bufs[r % 2], bufs[(r + 1) % 2], ms, r, r + 1 if r + 1 < kpt else -1)

    for c in every:
        rows = slice(c * qb, (c + 1) * qb)
        o = (acc_scr[c, 0:D_VB, :] / acc_scr[c, D_VB:D_VB + 1, :]).T
        o_ref[0, rows, :] = (o * _silu(z_ref[0, rows, :].astype(F32))).astype(BF16)


def _attention(q_t, k, v_t, proj3, *, offs):
    batch, n_heads, _, seq = q_t.shape
    tq = _pick(seq, (ATTN_TILE, ATTN_TILE // 2, ATTN_TILE // 4, ATTN_TILE // 8))
    qb = min(ATTN_QBLOCK, tq // 4)
    kb, nsub = 2 * qb, tq // qb
    assert offs["zb"] % D_VB == 0
    zb0 = offs["zb"] // D_VB
    kern = functools.partial(_attn_kernel, qb=qb, kb=kb, nsub=nsub)
    return pl.pallas_call(
        kern,
        grid=(batch, n_heads, seq // tq),
        in_specs=[
            pl.BlockSpec((1, 1, D_HQK, tq), lambda b, h, i: (b, h, 0, i)),
            pl.BlockSpec((1, 1, seq, D_HQK), lambda b, h, i: (b, h, 0, 0)),
            pl.BlockSpec((1, D_VB, seq), lambda b, h, i: (b, h, 0)),
            pl.BlockSpec((1, tq, D_VB), lambda b, h, i: (b, i, zb0 + h)),
        ],
        out_specs=pl.BlockSpec((1, tq, D_VB), lambda b, h, i: (b, i, h)),
        out_shape=jax.ShapeDtypeStruct((batch, seq, n_heads * D_VB), BF16),
        scratch_shapes=[
            pltpu.VMEM((nsub, kb, qb), F32),
            pltpu.VMEM((nsub, kb, qb), F32),
            pltpu.VMEM((nsub, D_VB + ATTN_PAD_ROWS, qb), F32),
        ],
        compiler_params=_cparams("parallel", "parallel", "arbitrary"),
    )(q_t, k, v_t, proj3)


def _merge_kernel(ha_ref, ob_ref, wa_ref, wb_ref, ga_ref, gb_ref, y_ref):
    ya = jnp.dot(ha_ref[...], wa_ref[...], preferred_element_type=F32)
    yb = jnp.dot(ob_ref[...], wb_ref[...], preferred_element_type=F32)
    y = _sigmoid(ga_ref[...].astype(F32)) * ya + _sigmoid(gb_ref[...].astype(F32)) * yb
    y_ref[...] = y.astype(BF16)


def _merge(ha, ob, w_a, w_b, proj, *, offs, layer):
    t, da = ha.shape
    db = ob.shape[1]
    d = w_a.shape[-1]
    bm = _pick(t, (1024, 512, 256, 128))
    bn = _pick(d, (1024, 512, 256, 128))
    assert offs["ga"] % bn == 0 and offs["gb"] % bn == 0
    ga0, gb0 = offs["ga"] // bn, offs["gb"] // bn
    return pl.pallas_call(
        _merge_kernel,
        grid=(t // bm, d // bn),
        in_specs=[
            pl.BlockSpec((bm, da), lambda i, j: (i, 0)),
            pl.BlockSpec((bm, db), lambda i, j: (i, 0)),
            pl.BlockSpec((None, da, bn), lambda i, j: (layer, 0, j)),
            pl.BlockSpec((None, db, bn), lambda i, j: (layer, 0, j)),
            pl.BlockSpec((bm, bn), lambda i, j: (i, ga0 + j)),
            pl.BlockSpec((bm, bn), lambda i, j: (i, gb0 + j)),
        ],
        out_specs=pl.BlockSpec((bm, bn), lambda i, j: (i, j)),
        out_shape=jax.ShapeDtypeStruct((t, d), BF16),
        compiler_params=_cparams("parallel", "arbitrary"),
    )(ha, ob, w_a, w_b, proj, proj)


def _outproj_kernel(y_ref, w_ref, x_ref, o_ref):
    o_ref[...] = x_ref[...] + jnp.dot(y_ref[...], w_ref[...], preferred_element_type=F32)


def _outproj(y, w_out, x2, *, layer):
    t, d = y.shape
    bm = _pick(t, (512, 256, 128))
    bn = _pick(d, (2048, 1024, 512, 256, 128))
    return pl.pallas_call(
        _outproj_kernel,
        grid=(t // bm, d // bn),
        in_specs=[
            pl.BlockSpec((bm, d), lambda i, j: (i, 0)),
            pl.BlockSpec((None, d, bn), lambda i, j: (layer, 0, j)),
            pl.BlockSpec((bm, bn), lambda i, j: (i, j)),
        ],
        out_specs=pl.BlockSpec((bm, bn), lambda i, j: (i, j)),
        out_shape=jax.ShapeDtypeStruct((t, d), F32),
        compiler_params=_cparams("parallel", "arbitrary"),
    )(y, w_out, x2)


def kernel(x, positions, norm_g, w_in, gate_bias, conv_w, mlstm_norm_g, w_a, q_lat_g, kv_lat_g, w_uq, w_ukv,
           q_norm_g, k_norm_g, w_b, w_out):
    batch, seq, d = x.shape
    depth = w_in.shape[0]
    h_a = gate_bias.shape[-1] // 2
    h_b = w_uq.shape[-1] // D_HQK
    qk_w, d_a, d_b = h_a * DQK_A, h_a * DV_A, h_b * D_VB
    ql, kvl = w_uq.shape[1], w_ukv.shape[1]
    assert 2 * h_a <= 16 and w_ukv.shape[-1] == h_b * (D_NOPE + D_VB)

    sizes = (qk_w, qk_w, d_a, d_a, h_a, h_a, d_a, ql, kvl, D_ROPE, d_b, d, d)
    names = ("q", "k", "v", "o", "i", "f", "za", "cq", "ckv", "kr", "zb", "ga", "gb")
    src, acc = {}, 0
    for nme, sz in zip(names, sizes):
        src[nme] = (acc, acc + sz)
        acc += sz
    assert acc == w_in.shape[-1]

    order = ("q", "k", "v", "o", "za", "cq", "ckv", "zb", "ga", "gb")
    offs, acc = {}, 0
    for nme in order:
        offs[nme] = acc
        acc += src[nme][1] - src[nme][0]
    runs = [(src[first][0], src[last][1]) for first, last in (("q", "o"), ("za", "ckv"), ("zb", "gb"))]
    bn = next(c for c in (1536, 1024, 512, 256, 128) if all((hi - lo) % c == 0 for lo, hi in runs))
    tile_rows = []
    for lo, hi in runs:
        assert lo % ROW_ALIGN == 0 and bn % ROW_ALIGN == 0
        tile_rows += [r // ROW_ALIGN for r in range(lo, hi, bn)]
    tile_rows = jnp.asarray(tile_rows, jnp.int32)

    w_t = jnp.swapaxes(w_in, 1, 2).astype(BF16)
    half = D_ROPE // 2
    kr0 = src["kr"][0]
    w_small_t = jnp.concatenate([
        w_t[:, kr0:kr0 + D_ROPE], w_t[:, kr0 + half:kr0 + D_ROPE], w_t[:, kr0:kr0 + half],
        w_t[:, src["i"][0]:src["f"][1]],
        jnp.zeros((depth, LANES - 2 * h_a, d), BF16)], axis=1)

    gb_col = jnp.pad(gate_bias, ((0, 0), (0, LANES - 2 * h_a)))[:, None, :]
    gb_row = jnp.pad(gate_bias, ((0, 0), (0, 16 - 2 * h_a)))[:, :, None]

    w_a_b, w_b_b, w_out_b = w_a.astype(BF16), w_b.astype(BF16), w_out.astype(BF16)
    wuq_t = jnp.swapaxes(w_uq, 1, 2).astype(BF16)
    w_ukv4 = w_ukv.reshape(depth, kvl, h_b, D_NOPE + D_VB)
    wk = w_ukv4[..., :D_NOPE].reshape(depth, kvl, h_b * D_NOPE).astype(BF16)
    wv_t = jnp.swapaxes(w_ukv4[..., D_NOPE:].reshape(depth, kvl, h_b * D_VB), 1, 2).astype(BF16)
    qn_g = q_norm_g[:, :, None]
    kn_g = k_norm_g[:, None, :]
    kn_g_sw = jnp.concatenate([k_norm_g[:, D_NOPE + half:], k_norm_g[:, D_NOPE:D_NOPE + half]], axis=-1)[:, None, :]

    inv_freq = jnp.exp(-math.log(ROPE_THETA) * jnp.arange(0, D_ROPE, 2, dtype=F32) / D_ROPE)
    ang = positions.astype(F32)[..., None] * inv_freq
    cos, sin = jnp.cos(ang), jnp.sin(ang)
    cos_t, sin_t = jnp.swapaxes(cos, 1, 2), jnp.swapaxes(sin, 1, 2)
    cos_k = jnp.concatenate([cos, cos], axis=-1)
    sin_k = jnp.concatenate([-sin, sin], axis=-1)

    x2 = x.reshape(batch * seq, d)
    for l in range(depth):
        proj, small, gates_t = _inproj(x2, norm_g[l][None, :], w_t, tile_rows, w_small_t, layer=l, bn=bn)
        h_gated = _mlstm(proj, small, gates_t, conv_w[l], gb_col[l], gb_row[l], mlstm_norm_g[l][None, :],
                         batch=batch, seq=seq, n_heads=h_a, offs=offs)
        proj3 = proj.reshape(batch, seq, -1)
        small3 = small.reshape(batch, seq, -1)
        q_t, k, v_t = _mla_prep(proj3, small3, cos_t, sin_t, cos_k, sin_k, q_lat_g[l][None, :], kv_lat_g[l][None, :],
                                wuq_t, wk, wv_t, qn_g[l], kn_g[l], kn_g_sw[l], n_heads=h_b, offs=offs, layer=l)
        o_gated = _attention(q_t, k, v_t, proj3, offs=offs).reshape(batch * seq, d_b)
        y = _merge(h_gated, o_gated, w_a_b, w_b_b, proj, offs=offs, layer=l)
        x2 = _outproj(y, w_out_b, x2, layer=l)
    return x2.reshape(batch, seq, d)
```

```python
import functools
import math

import jax
import jax.numpy as jnp
from jax import lax
from jax.experimental import pallas as pl
from jax.experimental.pallas import tpu as pltpu

F32 = jnp.float32
BF16 = jnp.bfloat16

DQK_A = 128
DV_A = 256
CONV_K = 4
D_NOPE = 128
D_ROPE = 64
D_HQK = D_NOPE + D_ROPE
D_VB = 128
ROPE_THETA = 10000.0
EPS = 1e-6
NEG_INF = -1e30
LOG2_E = 1.4426950408889634

LANES = 128
ROW_ALIGN = 16
VMEM_LIMIT_BYTES = 56 * 1024 * 1024

MLSTM_CHUNK = 256
CONV_SLAB = 256
ATTN_TILE = 4096
ATTN_QBLOCK = 256
ATTN_PAD_ROWS = 16
PREP_ROWS = 256
NORM_ROWS = 256


def _pick(n, candidates):
    for c in candidates:
        if n % c == 0:
            return c
    raise ValueError(f"no tile in {candidates} divides {n}")


def _cparams(*sem):
    return pltpu.CompilerParams(dimension_semantics=sem, vmem_limit_bytes=VMEM_LIMIT_BYTES)


def _sigmoid(x):
    return 1.0 / (1.0 + jnp.exp2(x * (-LOG2_E)))


def _silu(x):
    return x * _sigmoid(x)


def _log_sigmoid(x):
    return jnp.minimum(x, 0.0) - jnp.log(1.0 + jnp.exp(-jnp.abs(x)))


def _prefix_max_rows(x):
    n = x.shape[0]
    rows = lax.broadcasted_iota(jnp.int32, x.shape, 0)
    k = 1
    while k < n:
        if k < 8:
            shifted = jnp.where(rows < k, NEG_INF, pltpu.roll(x, k, 0))
        else:
            shifted = jnp.concatenate([jnp.full((k, x.shape[1]), NEG_INF, x.dtype), x[:n - k, :]], axis=0)
        x = jnp.maximum(x, shifted)
        k *= 2
    return x


def _dot_nt(a, b):
    return lax.dot_general(a, b, (((1,), (1,)), ((), ())), preferred_element_type=F32)


def _dot_tn(a, b):
    return lax.dot_general(a, b, (((0,), (0,)), ((), ())), preferred_element_type=F32)


def _inproj_kernel(rows_ref, x_ref, g_ref, wt_ref, wst_ref, o_ref, os_ref, ogt_ref, h_scr):
    del rows_ref
    @pl.when(pl.program_id(1) == 0)
    def _():
        rows = x_ref.shape[0]
        step = min(NORM_ROWS, rows)
        for r0 in range(0, rows, step):
            x = x_ref[r0:r0 + step, :]
            ms = jnp.mean(x * x, axis=-1, keepdims=True)
            h_scr[r0:r0 + step, :] = (x * lax.rsqrt(ms + EPS) * g_ref[...]).astype(BF16)
        small = _dot_nt(h_scr[...], wst_ref[...])
        os_ref[...] = small
        ogt_ref[...] = small[:, LANES:2 * LANES].T[0:16, :]

    o_ref[...] = _dot_nt(h_scr[...], wt_ref[0]).astype(BF16)


def _inproj(x2, norm_g, w_t, tile_rows, w_small_t, *, layer, bn):
    t, d = x2.shape
    n_tiles = tile_rows.shape[0]
    nm = n_tiles * bn
    bm = _pick(t, (1024, 512, 256, 128))
    grid_spec = pltpu.PrefetchScalarGridSpec(
        num_scalar_prefetch=1,
        grid=(t // bm, n_tiles),
        in_specs=[
            pl.BlockSpec((bm, d), lambda i, j, rows: (i, 0)),
            pl.BlockSpec((1, d), lambda i, j, rows: (0, 0)),
            pl.BlockSpec((pl.Element(1), pl.Element(bn), pl.Element(d)),
                         lambda i, j, rows: (layer, rows[j] * ROW_ALIGN, 0)),
            pl.BlockSpec((None, 2 * LANES, d), lambda i, j, rows: (layer, 0, 0)),
        ],
        out_specs=[
            pl.BlockSpec((bm, bn), lambda i, j, rows: (i, j)),
            pl.BlockSpec((bm, 2 * LANES), lambda i, j, rows: (i, 0)),
            pl.BlockSpec((16, bm), lambda i, j, rows: (0, i)),
        ],
        scratch_shapes=[pltpu.VMEM((bm, d), BF16)],
    )
    return pl.pallas_call(
        _inproj_kernel,
        grid_spec=grid_spec,
        out_shape=[
            jax.ShapeDtypeStruct((t, nm), BF16),
            jax.ShapeDtypeStruct((t, 2 * LANES), F32),
            jax.ShapeDtypeStruct((16, t), F32),
        ],
        compiler_params=_cparams("parallel", "arbitrary"),
    )(tile_rows, x2, norm_g, w_t, w_small_t)


def _mlstm_kernel(q_ref, k_ref, v_ref, o_ref, z_ref, gc_ref, gr_ref, cw_ref, gbc_ref, gbr_ref, ng_ref,
                  out_ref, c_scr, m_scr, qk_scr, tail_scr, s_scr, cb_scr, *, n_heads, chunk):
    L = chunk
    qk_w = n_heads * DQK_A
    c_idx = pl.program_id(1)

    @pl.when(c_idx == 0)
    def _():
        c_scr[...] = jnp.zeros_like(c_scr)
        m_scr[...] = jnp.zeros_like(m_scr)
        tail_scr[...] = jnp.zeros_like(tail_scr)

    row = lax.broadcasted_iota(jnp.int32, (L, L), 0)
    col = lax.broadcasted_iota(jnp.int32, (L, L), 1)

    shifts = [(row - col == s).astype(BF16) for s in range(1, CONV_K)]
    sub8 = lax.broadcasted_iota(jnp.int32, (8, CONV_SLAB), 0)

    def conv_slab(slab):
        cs = slice(slab * CONV_SLAB, (slab + 1) * CONV_SLAB)
        src_ref, lo = (q_ref, slab * CONV_SLAB) if slab * CONV_SLAB < qk_w else (k_ref, slab * CONV_SLAB - qk_w)
        x_b = src_ref[:, lo:lo + CONV_SLAB]
        x_f = x_b.astype(F32)
        tail = tail_scr[:, cs]
        acc = x_f * cw_ref[CONV_K - 1:CONV_K, cs]
        head_fix = jnp.zeros((8, CONV_SLAB), F32)
        for s in range(1, CONV_K):
            w_s = cw_ref[CONV_K - 1 - s:CONV_K - s, cs]
            acc = acc + jnp.dot(shifts[s - 1], x_b, preferred_element_type=F32) * w_s
            head_fix = head_fix + jnp.where(sub8 < s, pltpu.roll(tail, s, 0), 0.0) * w_s
        tail_scr[:, cs] = x_f[L - 8:L, :]
        qk_scr[8:L, cs] = _silu(acc[8:L, :])
        qk_scr[0:8, cs] = _silu(acc[0:8, :] + head_fix)

    ones_cols = jnp.ones((L, LANES), BF16)

    def head_slices(h):
        return (slice(h * DQK_A, (h + 1) * DQK_A), slice(qk_w + h * DQK_A, qk_w + (h + 1) * DQK_A),
                slice(h * DV_A, (h + 1) * DV_A))

    gc = gc_ref[...] + gbc_ref[...]
    gr = gr_ref[...] + gbr_ref[...]
    causal = col <= row
    tri = causal.astype(F32)
    tri_t = (row <= col).astype(F32)
    b_all = jnp.dot(tri, _log_sigmoid(gc), preferred_element_type=F32, precision=lax.Precision.HIGHEST)
    b_r_all = jnp.dot(_log_sigmoid(gr), tri_t, preferred_element_type=F32, precision=lax.Precision.HIGHEST)
    b_al = pltpu.roll(b_all, LANES - n_heads, 1)
    m_old = m_scr[0:1, :]
    mx = jnp.maximum(_prefix_max_rows(gc - b_al), m_old)
    mx2 = mx * LOG2_E
    decay_all = jnp.exp(m_old - mx) * (DQK_A ** -0.5)
    enm_all = jnp.exp(-(b_al + mx))
    g_tot = b_al[L - 1:L, :]
    w_all = g_tot - b_al + gc
    m_new = jnp.maximum(g_tot + m_old, jnp.max(w_all, axis=0, keepdims=True))
    carry_all = jnp.exp(g_tot + m_old - m_new)
    ew_all = jnp.exp(w_all - m_new)
    m_scr[0:1, :] = m_new

    def stage1(h):
        qs, ks, vs = head_slices(h)
        k_f = qk_scr[:, ks]
        v_ext = jnp.concatenate([v_ref[:, vs], ones_cols], axis=1)
        s_scr[h] = _dot_nt(qk_scr[:, qs].astype(BF16), k_f.astype(BF16))
        c_prev = c_scr[h]
        cb_scr[h] = c_prev.astype(BF16)
        kw = k_f * ew_all[:, h:h + 1]
        c_scr[h] = carry_all[:, h:h + 1] * c_prev + _dot_tn(kw.astype(BF16), v_ext)

    def stage2(h):
        qs, _, vs = head_slices(h)
        v_ext = jnp.concatenate([v_ref[:, vs], ones_cols], axis=1)
        r_row = gr[h:h + 1, :] - b_r_all[n_heads + h:n_heads + h + 1, :]
        r2_row = r_row * LOG2_E + math.log2(DQK_A ** -0.5)
        p = s_scr[h] * jnp.exp2(jnp.where(causal, r2_row - mx2[:, h:h + 1], NEG_INF))
        q_dec = qk_scr[:, qs] * decay_all[:, h:h + 1]
        lhs = jnp.concatenate([p.astype(BF16), q_dec.astype(BF16)], axis=1)
        rhs = jnp.concatenate([v_ext, cb_scr[h]], axis=0)
        num_den = jnp.dot(lhs, rhs, preferred_element_type=F32)
        den = num_den[:, DV_A:DV_A + LANES]
        inv = 1.0 / jnp.maximum(jnp.abs(den), enm_all[:, h:h + 1])
        hh = num_den[:, 0:DV_A] * jnp.concatenate([inv] * (DV_A // LANES), axis=1)

        ms = jnp.mean(hh * hh, axis=-1, keepdims=True)
        hn = hh * lax.rsqrt(ms + EPS) * ng_ref[0:1, vs]
        gated = _sigmoid(o_ref[:, vs].astype(F32)) * hn * _silu(z_ref[:, vs].astype(F32))
        out_ref[:, vs] = gated.astype(BF16)

    for slab in range(2 * qk_w // CONV_SLAB):
        conv_slab(slab)
    for h in range(n_heads):
        stage1(h)
    for h in range(n_heads):
        stage2(h)


def _mlstm(proj, small, gates_t, conv_w, gb_col, gb_row, norm_g, *, batch, seq, n_heads, offs):
    t = batch * seq
    L = _pick(seq, (MLSTM_CHUNK, 128, 64))
    nc = seq // L
    qk_w = n_heads * DQK_A
    dv_w = n_heads * DV_A
    assert offs["q"] % qk_w == 0 and offs["k"] % qk_w == 0
    assert offs["v"] % dv_w == 0 and offs["o"] % dv_w == 0 and offs["za"] % dv_w == 0
    qb, kb = offs["q"] // qk_w, offs["k"] // qk_w
    vb, ob, zb = offs["v"] // dv_w, offs["o"] // dv_w, offs["za"] // dv_w
    kern = functools.partial(_mlstm_kernel, n_heads=n_heads, chunk=L)
    return pl.pallas_call(
        kern,
        grid=(batch, nc),
        in_specs=[
            pl.BlockSpec((L, qk_w), lambda b, c: (b * nc + c, qb)),
            pl.BlockSpec((L, qk_w), lambda b, c: (b * nc + c, kb)),
            pl.BlockSpec((L, dv_w), lambda b, c: (b * nc + c, vb)),
            pl.BlockSpec((L, dv_w), lambda b, c: (b * nc + c, ob)),
            pl.BlockSpec((L, dv_w), lambda b, c: (b * nc + c, zb)),
            pl.BlockSpec((L, LANES), lambda b, c: (b * nc + c, 1)),
            pl.BlockSpec((16, L), lambda b, c: (0, b * nc + c)),
            pl.BlockSpec((CONV_K, 2 * qk_w), lambda b, c: (0, 0)),
            pl.BlockSpec((1, LANES), lambda b, c: (0, 0)),
            pl.BlockSpec((16, 1), lambda b, c: (0, 0)),
            pl.BlockSpec((1, dv_w), lambda b, c: (0, 0)),
        ],
        out_specs=pl.BlockSpec((L, dv_w), lambda b, c: (b * nc + c, 0)),
        out_shape=jax.ShapeDtypeStruct((t, dv_w), BF16),
        scratch_shapes=[
            pltpu.VMEM((n_heads, DQK_A, DV_A + LANES), F32),
            pltpu.VMEM((8, LANES), F32),
            pltpu.VMEM((L, 2 * qk_w), F32),
            pltpu.VMEM((8, 2 * qk_w), F32),
            pltpu.VMEM((n_heads, L, L), F32),
            pltpu.VMEM((n_heads, DQK_A, DV_A + LANES), BF16),
        ],
        compiler_params=_cparams("parallel", "arbitrary"),
    )(proj, proj, proj, proj, proj, small, gates_t, conv_w, gb_col, gb_row, norm_g)


def _mla_prep_kernel(cq_ref, ckv_ref, kr_ref, cost_ref, sint_ref, cosk_ref, sink_ref, qlg_ref, kvlg_ref,
                     wuqt_ref, wk_ref, wvt_ref, qng_ref, kng_ref, kngs_ref, qt_ref, k_ref, vt_ref, *, n_heads):
    cq = cq_ref[...].astype(F32)
    cqn = (cq * lax.rsqrt(jnp.mean(cq * cq, axis=-1, keepdims=True) + EPS) * qlg_ref[...]).astype(BF16)
    ckv = ckv_ref[...].astype(F32)
    ckvn = (ckv * lax.rsqrt(jnp.mean(ckv * ckv, axis=-1, keepdims=True) + EPS) * kvlg_ref[...]).astype(BF16)

    cos_t = cost_ref[0]
    sin_t = sint_ref[0]
    half = D_ROPE // 2
    q_t = _dot_nt(wuqt_ref[...], cqn)
    for h in range(n_heads):
        qh = q_t[h * D_HQK:(h + 1) * D_HQK, :]
        ss = jnp.sum(qh * qh, axis=0, keepdims=True)
        rs = lax.rsqrt(ss * (1.0 / D_HQK) + EPS) * (D_HQK ** -0.5 * LOG2_E)
        qn = qh * rs * qng_ref[...]
        x1 = qn[D_NOPE:D_NOPE + half, :]
        x2 = qn[D_NOPE + half:D_HQK, :]
        qt_ref[0, h, 0:D_NOPE, :] = qn[0:D_NOPE, :].astype(BF16)
        qt_ref[0, h, D_NOPE:D_NOPE + half, :] = (x1 * cos_t - x2 * sin_t).astype(BF16)
        qt_ref[0, h, D_NOPE + half:D_HQK, :] = (x2 * cos_t + x1 * sin_t).astype(BF16)

    kn = jnp.dot(ckvn, wk_ref[...], preferred_element_type=F32)
    kr = kr_ref[:, 0:D_ROPE]
    kr_sw = kr_ref[:, D_ROPE:2 * D_ROPE]
    ssr = jnp.sum(kr * kr, axis=-1, keepdims=True)
    k_rot = kr * kng_ref[:, D_NOPE:D_HQK] * cosk_ref[0] + kr_sw * kngs_ref[...] * sink_ref[0]
    for h in range(n_heads):
        kh = kn[:, h * D_NOPE:(h + 1) * D_NOPE]
        ss = jnp.sum(kh * kh, axis=-1, keepdims=True) + ssr
        rs = lax.rsqrt(ss * (1.0 / D_HQK) + EPS)
        k_ref[0, h, :, 0:D_NOPE] = (kh * rs * kng_ref[:, 0:D_NOPE]).astype(BF16)
        k_ref[0, h, :, D_NOPE:D_HQK] = (k_rot * rs).astype(BF16)

    vt_ref[0] = _dot_nt(wvt_ref[...], ckvn).astype(BF16)


def _mla_prep(proj3, small3, cos_t, sin_t, cos_k, sin_k, q_lat_g, kv_lat_g, wuq_t, wk, wv_t, qn_g, kn_g, kn_g_sw,
              *, n_heads, offs, layer):
    batch, seq, _ = proj3.shape
    ql, kvl = wuq_t.shape[-1], wk.shape[-2]
    ts = _pick(seq, (PREP_ROWS, 128))
    assert offs["cq"] % ql == 0 and offs["ckv"] % kvl == 0
    cqb, ckvb = offs["cq"] // ql, offs["ckv"] // kvl
    half = D_ROPE // 2
    kern = functools.partial(_mla_prep_kernel, n_heads=n_heads)
    const = lambda b, s: (0, 0)
    return pl.pallas_call(
        kern,
        grid=(batch, seq // ts),
        in_specs=[
            pl.BlockSpec((None, ts, ql), lambda b, s: (b, s, cqb)),
            pl.BlockSpec((None, ts, kvl), lambda b, s: (b, s, ckvb)),
            pl.BlockSpec((None, ts, LANES), lambda b, s: (b, s, 0)),
            pl.BlockSpec((1, half, ts), lambda b, s: (b, 0, s)),
            pl.BlockSpec((1, half, ts), lambda b, s: (b, 0, s)),
            pl.BlockSpec((1, ts, D_ROPE), lambda b, s: (b, s, 0)),
            pl.BlockSpec((1, ts, D_ROPE), lambda b, s: (b, s, 0)),
            pl.BlockSpec((1, ql), const),
            pl.BlockSpec((1, kvl), const),
            pl.BlockSpec((None, n_heads * D_HQK, ql), lambda b, s: (layer, 0, 0)),
            pl.BlockSpec((None, kvl, n_heads * D_NOPE), lambda b, s: (layer, 0, 0)),
            pl.BlockSpec((None, n_heads * D_VB, kvl), lambda b, s: (layer, 0, 0)),
            pl.BlockSpec((D_HQK, 1), const),
            pl.BlockSpec((1, D_HQK), const),
            pl.BlockSpec((1, D_ROPE), const),
        ],
        out_specs=[
            pl.BlockSpec((1, n_heads, D_HQK, ts), lambda b, s: (b, 0, 0, s)),
            pl.BlockSpec((1, n_heads, ts, D_HQK), lambda b, s: (b, 0, s, 0)),
            pl.BlockSpec((1, n_heads * D_VB, ts), lambda b, s: (b, 0, s)),
        ],
        out_shape=[
            jax.ShapeDtypeStruct((batch, n_heads, D_HQK, seq), BF16),
            jax.ShapeDtypeStruct((batch, n_heads, seq, D_HQK), BF16),
            jax.ShapeDtypeStruct((batch, n_heads * D_VB, seq), BF16),
        ],
        compiler_params=_cparams("parallel", "parallel"),
    )(proj3, proj3, small3, cos_t, sin_t, cos_k, sin_k, q_lat_g, kv_lat_g, wuq_t, wk, wv_t, qn_g, kn_g, kn_g_sw)


def _attn_kernel(qt_ref, k_ref, vt_ref, z_ref, o_ref, sa_scr, sb_scr, acc_scr, *, qb, kb, nsub):
    qi = pl.program_id(2)
    kpt = nsub * qb // kb
    assert kpt % 2 == 0
    base = qi * kpt
    every = tuple(range(nsub))
    ones_rows = {n: jnp.where(lax.broadcasted_iota(jnp.int32, (ATTN_PAD_ROWS, n), 0) == 0, 1.0, 0.0).astype(BF16)
                 for n in (qb, kb)}

    def visible(c, diag_r):
        if diag_r is None:
            return kb
        lead = c * qb - diag_r * kb
        return min(max(lead + qb, 0), kb)

    def scores(j, dst, c, n_keys):
        start = pl.multiple_of(j * kb, kb)
        k_blk = k_ref[0, 0, pl.ds(start, n_keys), :]
        q_t = qt_ref[0, 0, :, c * qb:(c + 1) * qb]
        dst[c, 0:n_keys, :] = jnp.dot(k_blk, q_t, preferred_element_type=F32)

    def consume(j, src, ms, c, diag_r):
        n_keys = visible(c, diag_r)
        start = pl.multiple_of(j * kb, kb)
        v_ext = jnp.concatenate([vt_ref[0, :, pl.ds(start, n_keys)], ones_rows[n_keys]], axis=0)
        s = src[c, 0:n_keys, :]
        if diag_r is not None and c * qb - diag_r * kb < n_keys - 1:
            kk = lax.broadcasted_iota(jnp.int32, (n_keys, qb), 0)
            qq = lax.broadcasted_iota(jnp.int32, (n_keys, qb), 1)
            s = jnp.where(kk <= qq + (c * qb - diag_r * kb), s, NEG_INF)
        m_new = jnp.maximum(ms[c], jnp.max(s, axis=0, keepdims=True))
        p = jnp.exp2(s - m_new).astype(BF16)
        alpha = jnp.exp2(ms[c] - m_new)
        acc_scr[c] = alpha * acc_scr[c] + jnp.dot(v_ext, p, preferred_element_type=F32)
        return ms[:c] + (m_new,) + ms[c + 1:]

    def phase(j_cur, src, dst, ms, diag_r, next_r):
        for c in every:
            if next_r != -1 and visible(c, next_r) > 0:
                scores(j_cur + 1, dst, c, visible(c, next_r))
            if visible(c, diag_r) > 0:
                ms = consume(j_cur, src, ms, c, diag_r)
        return ms

    acc_scr[...] = jnp.zeros_like(acc_scr)
    ms = tuple(jnp.full((1, qb), NEG_INF, F32) for _ in every)

    for c in every:
        scores(0, sa_scr, c, kb)

    def body(jj, ms):
        j0 = 2 * jj
        ms = phase(j0, sa_scr, sb_scr, ms, None, None)
        return phase(j0 + 1, sb_scr, sa_scr, ms, None, None)

    ms = lax.fori_loop(0, base // 2, body, ms)

    bufs = (sa_scr, sb_scr)
    for r in range(kpt):
        ms = phase(base + r, bufs[r % 2], bufs[(r + 1) % 2], ms, r, r + 1 if r + 1 < kpt else -1)

    for c in every:
        rows = slice(c * qb, (c + 1) * qb)
        o = (acc_scr[c, 0:D_VB, :] / acc_scr[c, D_VB:D_VB + 1, :]).T
        o_ref[0, rows, :] = (o * _silu(z_ref[0, rows, :].astype(F32))).astype(BF16)


def _attention(q_t, k, v_t, proj3, *, offs):
    batch, n_heads, _, seq = q_t.shape
    tq = _pick(seq, (ATTN_TILE, ATTN_TILE // 2, ATTN_TILE // 4, ATTN_TILE // 8))
    qb = min(ATTN_QBLOCK, tq // 4)
    kb, nsub = 2 * qb, tq // qb
    assert offs["zb"] % D_VB == 0
    zb0 = offs["zb"] // D_VB
    kern = functools.partial(_attn_kernel, qb=qb, kb=kb, nsub=nsub)
    return pl.pallas_call(
        kern,
        grid=(batch, n_heads, seq // tq),
        in_specs=[
            pl.BlockSpec((1, 1, D_HQK, tq), lambda b, h, i: (b, h, 0, i)),
            pl.BlockSpec((1, 1, seq, D_HQK), lambda b, h, i: (b, h, 0, 0)),
            pl.BlockSpec((1, D_VB, seq), lambda b, h, i: (b, h, 0)),
            pl.BlockSpec((1, tq, D_VB), lambda b, h, i: (b, i, zb0 + h)),
        ],
        out_specs=pl.BlockSpec((1, tq, D_VB), lambda b, h, i: (b, i, h)),
        out_shape=jax.ShapeDtypeStruct((batch, seq, n_heads * D_VB), BF16),
        scratch_shapes=[
            pltpu.VMEM((nsub, kb, qb), F32),
            pltpu.VMEM((nsub, kb, qb), F32),
            pltpu.VMEM((nsub, D_VB + ATTN_PAD_ROWS, qb), F32),
        ],
        compiler_params=_cparams("parallel", "parallel", "arbitrary"),
    )(q_t, k, v_t, proj3)


def _merge_kernel(ha_ref, ob_ref, wa_ref, wb_ref, *rest):
    n = (len(rest) - 1) // 2
    ga_refs, gb_refs, y_ref = rest[:n], rest[n:2 * n], rest[2 * n]
    w = ga_refs[0].shape[1]
    ya = jnp.dot(ha_ref[...], wa_ref[...], preferred_element_type=F32)
    yb = jnp.dot(ob_ref[...], wb_ref[...], preferred_element_type=F32)
    for c in range(n):
        cols = slice(c * w, (c + 1) * w)
        y = (_sigmoid(ga_refs[c][...].astype(F32)) * ya[:, cols]
             + _sigmoid(gb_refs[c][...].astype(F32)) * yb[:, cols])
        y_ref[:, cols] = y.astype(BF16)


def _merge(ha, ob, w_a, w_b, proj, *, offs, layer):
    t, da = ha.shape
    db = ob.shape[1]
    d = w_a.shape[-1]
    bm = _pick(t, (512, 256, 128))
    gw = next(c for c in (2048, 1024, 512, 256, 128) if d % c == 0 and offs["ga"] % c == 0 and offs["gb"] % c == 0)
    n = d // gw
    gate_specs = [pl.BlockSpec((bm, gw), lambda i, c=c, g=g: (i, offs[g] // gw + c))
                  for g in ("ga", "gb") for c in range(n)]
    return pl.pallas_call(
        _merge_kernel,
        grid=(t // bm,),
        in_specs=[
            pl.BlockSpec((bm, da), lambda i: (i, 0)),
            pl.BlockSpec((bm, db), lambda i: (i, 0)),
            pl.BlockSpec((None, da, d), lambda i: (layer, 0, 0)),
            pl.BlockSpec((None, db, d), lambda i: (layer, 0, 0)),
        ] + gate_specs,
        out_specs=pl.BlockSpec((bm, d), lambda i: (i, 0)),
        out_shape=jax.ShapeDtypeStruct((t, d), BF16),
        compiler_params=_cparams("parallel"),
    )(ha, ob, w_a, w_b, *([proj] * (2 * n)))


def _outproj_kernel(y_ref, w_ref, x_ref, o_ref):
    o_ref[...] = x_ref[...] + jnp.dot(y_ref[...], w_ref[...], preferred_element_type=F32)


def _outproj(y, w_out, x2, *, layer):
    t, d = y.shape
    bm = _pick(t, (512, 256, 128))
    bn = _pick(d, (2048, 1024, 512, 256, 128))
    return pl.pallas_call(
        _outproj_kernel,
        grid=(t // bm, d // bn),
        in_specs=[
            pl.BlockSpec((bm, d), lambda i, j: (i, 0)),
            pl.BlockSpec((None, d, bn), lambda i, j: (layer, 0, j)),
            pl.BlockSpec((bm, bn), lambda i, j: (i, j)),
        ],
        out_specs=pl.BlockSpec((bm, bn), lambda i, j: (i, j)),
        out_shape=jax.ShapeDtypeStruct((t, d), F32),
        compiler_params=_cparams("parallel", "arbitrary"),
    )(y, w_out, x2)


def kernel(x, positions, norm_g, w_in, gate_bias, conv_w, mlstm_norm_g, w_a, q_lat_g, kv_lat_g, w_uq, w_ukv,
           q_norm_g, k_norm_g, w_b, w_out):
    batch, seq, d = x.shape
    depth = w_in.shape[0]
    h_a = gate_bias.shape[-1] // 2
    h_b = w_uq.shape[-1] // D_HQK
    qk_w, d_a, d_b = h_a * DQK_A, h_a * DV_A, h_b * D_VB
    ql, kvl = w_uq.shape[1], w_ukv.shape[1]
    assert 2 * h_a <= 16 and w_ukv.shape[-1] == h_b * (D_NOPE + D_VB)

    sizes = (qk_w, qk_w, d_a, d_a, h_a, h_a, d_a, ql, kvl, D_ROPE, d_b, d, d)
    names = ("q", "k", "v", "o", "i", "f", "za", "cq", "ckv", "kr", "zb", "ga", "gb")
    src, acc = {}, 0
    for nme, sz in zip(names, sizes):
        src[nme] = (acc, acc + sz)
        acc += sz
    assert acc == w_in.shape[-1]

    order = ("q", "k", "v", "o", "za", "cq", "ckv", "zb", "ga", "gb")
    offs, acc = {}, 0
    for nme in order:
        offs[nme] = acc
        acc += src[nme][1] - src[nme][0]
    runs = [(src[first][0], src[last][1]) for first, last in (("q", "o"), ("za", "ckv"), ("zb", "gb"))]
    bn = next(c for c in (1536, 1024, 512, 256, 128) if all((hi - lo) % c == 0 for lo, hi in runs))
    tile_rows = []
    for lo, hi in runs:
        assert lo % ROW_ALIGN == 0 and bn % ROW_ALIGN == 0
        tile_rows += [r // ROW_ALIGN for r in range(lo, hi, bn)]
    tile_rows = jnp.asarray(tile_rows, jnp.int32)

    w_t = jnp.swapaxes(w_in, 1, 2).astype(BF16)
    half = D_ROPE // 2
    kr0 = src["kr"][0]
    w_small_t = jnp.concatenate([
        w_t[:, kr0:kr0 + D_ROPE], w_t[:, kr0 + half:kr0 + D_ROPE], w_t[:, kr0:kr0 + half],
        w_t[:, src["i"][0]:src["f"][1]],
        jnp.zeros((depth, LANES - 2 * h_a, d), BF16)], axis=1)

    gb_col = jnp.pad(gate_bias, ((0, 0), (0, LANES - 2 * h_a)))[:, None, :]
    gb_row = jnp.pad(gate_bias, ((0, 0), (0, 16 - 2 * h_a)))[:, :, None]

    w_a_b, w_b_b, w_out_b = w_a.astype(BF16), w_b.astype(BF16), w_out.astype(BF16)
    wuq_t = jnp.swapaxes(w_uq, 1, 2).astype(BF16)
    w_ukv4 = w_ukv.reshape(depth, kvl, h_b, D_NOPE + D_VB)
    wk = w_ukv4[..., :D_NOPE].reshape(depth, kvl, h_b * D_NOPE).astype(BF16)
    wv_t = jnp.swapaxes(w_ukv4[..., D_NOPE:].reshape(depth, kvl, h_b * D_VB), 1, 2).astype(BF16)
    qn_g = q_norm_g[:, :, None]
    kn_g = k_norm_g[:, None, :]
    kn_g_sw = jnp.concatenate([k_norm_g[:, D_NOPE + half:], k_norm_g[:, D_NOPE:D_NOPE + half]], axis=-1)[:, None, :]

    inv_freq = jnp.exp(-math.log(ROPE_THETA) * jnp.arange(0, D_ROPE, 2, dtype=F32) / D_ROPE)
    ang = positions.astype(F32)[..., None] * inv_freq
    cos, sin = jnp.cos(ang), jnp.sin(ang)
    cos_t, sin_t = jnp.swapaxes(cos, 1, 2), jnp.swapaxes(sin, 1, 2)
    cos_k = jnp.concatenate([cos, cos], axis=-1)
    sin_k = jnp.concatenate([-sin, sin], axis=-1)

    x2 = x.reshape(batch * seq, d)
    for l in range(depth):
        proj, small, gates_t = _inproj(x2, norm_g[l][None, :], w_t, tile_rows, w_small_t, layer=l, bn=bn)
        h_gated = _mlstm(proj, small, gates_t, conv_w[l], gb_col[l], gb_row[l], mlstm_norm_g[l][None, :],
                         batch=batch, seq=seq, n_heads=h_a, offs=offs)
        proj3 = proj.reshape(batch, seq, -1)
        small3 = small.reshape(batch, seq, -1)
        q_t, k, v_t = _mla_prep(proj3, small3, cos_t, sin_t, cos_k, sin_k, q_lat_g[l][None, :], kv_lat_g[l][None, :],
                                wuq_t, wk, wv_t, qn_g[l], kn_g[l], kn_g_sw[l], n_heads=h_b, offs=offs, layer=l)
        o_gated = _attention(q_t, k, v_t, proj3, offs=offs).reshape(batch * seq, d_b)
        y = _merge(h_gated, o_gated, w_a_b, w_b_b, proj, offs=offs, layer=l)
        x2 = _outproj(y, w_out_b, x2, layer=l)
    return x2.reshape(batch, seq, d)
```

```python
import functools
import math

import jax
import jax.numpy as jnp
from jax import lax
from jax.experimental import pallas as pl
from jax.experimental.pallas import tpu as pltpu

F32 = jnp.float32
BF16 = jnp.bfloat16

DQK_A = 128
DV_A = 256
CONV_K = 4
D_NOPE = 128
D_ROPE = 64
D_HQK = D_NOPE + D_ROPE
D_VB = 128
ROPE_THETA = 10000.0
EPS = 1e-6
NEG_INF = -1e30
LOG2_E = 1.4426950408889634

LANES = 128
ROW_ALIGN = 16
VMEM_LIMIT_BYTES = 56 * 1024 * 1024

MLSTM_CHUNK = 256
CONV_SLAB = 256
ATTN_TILE = 4096
ATTN_QBLOCK = 256
ATTN_PAD_ROWS = 16
PREP_ROWS = 256
NORM_ROWS = 256


def _pick(n, candidates):
    for c in candidates:
        if n % c == 0:
            return c
    raise ValueError(f"no tile in {candidates} divides {n}")


def _cparams(*sem):
    return pltpu.CompilerParams(dimension_semantics=sem, vmem_limit_bytes=VMEM_LIMIT_BYTES)


def _sigmoid(x):
    return 1.0 / (1.0 + jnp.exp2(x * (-LOG2_E)))


def _silu(x):
    return x * _sigmoid(x)


def _log_sigmoid(x):
    return jnp.minimum(x, 0.0) - jnp.log(1.0 + jnp.exp(-jnp.abs(x)))


def _prefix_max_rows(x):
    n = x.shape[0]
    rows = lax.broadcasted_iota(jnp.int32, x.shape, 0)
    k = 1
    while k < n:
        if k < 8:
            shifted = jnp.where(rows < k, NEG_INF, pltpu.roll(x, k, 0))
        else:
            shifted = jnp.concatenate([jnp.full((k, x.shape[1]), NEG_INF, x.dtype), x[:n - k, :]], axis=0)
        x = jnp.maximum(x, shifted)
        k *= 2
    return x


def _dot_nt(a, b):
    return lax.dot_general(a, b, (((1,), (1,)), ((), ())), preferred_element_type=F32)


def _dot_tn(a, b):
    return lax.dot_general(a, b, (((0,), (0,)), ((), ())), preferred_element_type=F32)


def _inproj_kernel(rows_ref, x_ref, g_ref, wt_ref, wst_ref, o_ref, os_ref, ogt_ref, h_scr):
    del rows_ref
    @pl.when(pl.program_id(1) == 0)
    def _():
        rows = x_ref.shape[0]
        step = min(NORM_ROWS, rows)
        for r0 in range(0, rows, step):
            x = x_ref[r0:r0 + step, :]
            ms = jnp.mean(x * x, axis=-1, keepdims=True)
            h_scr[r0:r0 + step, :] = (x * lax.rsqrt(ms + EPS) * g_ref[...]).astype(BF16)
        small = _dot_nt(h_scr[...], wst_ref[...])
        os_ref[...] = small
        ogt_ref[...] = small[:, LANES:2 * LANES].T[0:16, :]

    o_ref[...] = _dot_nt(h_scr[...], wt_ref[0]).astype(BF16)


def _inproj(x2, norm_g, w_t, tile_rows, w_small_t, *, layer, bn):
    t, d = x2.shape
    n_tiles = tile_rows.shape[0]
    nm = n_tiles * bn
    bm = _pick(t, (1024, 512, 256, 128))
    grid_spec = pltpu.PrefetchScalarGridSpec(
        num_scalar_prefetch=1,
        grid=(t // bm, n_tiles),
        in_specs=[
            pl.BlockSpec((bm, d), lambda i, j, rows: (i, 0)),
            pl.BlockSpec((1, d), lambda i, j, rows: (0, 0)),
            pl.BlockSpec((pl.Element(1), pl.Element(bn), pl.Element(d)),
                         lambda i, j, rows: (layer, rows[j] * ROW_ALIGN, 0)),
            pl.BlockSpec((None, 2 * LANES, d), lambda i, j, rows: (layer, 0, 0)),
        ],
        out_specs=[
            pl.BlockSpec((bm, bn), lambda i, j, rows: (i, j)),
            pl.BlockSpec((bm, 2 * LANES), lambda i, j, rows: (i, 0)),
            pl.BlockSpec((16, bm), lambda i, j, rows: (0, i)),
        ],
        scratch_shapes=[pltpu.VMEM((bm, d), BF16)],
    )
    return pl.pallas_call(
        _inproj_kernel,
        grid_spec=grid_spec,
        out_shape=[
            jax.ShapeDtypeStruct((t, nm), BF16),
            jax.ShapeDtypeStruct((t, 2 * LANES), F32),
            jax.ShapeDtypeStruct((16, t), F32),
        ],
        compiler_params=_cparams("parallel", "arbitrary"),
    )(tile_rows, x2, norm_g, w_t, w_small_t)


def _mlstm_kernel(q_ref, k_ref, v_ref, o_ref, z_ref, gc_ref, gr_ref, cw_ref, gbc_ref, gbr_ref, ng_ref,
                  out_ref, c_scr, m_scr, qk_scr, tail_scr, s_scr, cb_scr, *, n_heads, chunk):
    L = chunk
    qk_w = n_heads * DQK_A
    c_idx = pl.program_id(1)

    @pl.when(c_idx == 0)
    def _():
        c_scr[...] = jnp.zeros_like(c_scr)
        m_scr[...] = jnp.zeros_like(m_scr)
        tail_scr[...] = jnp.zeros_like(tail_scr)

    row = lax.broadcasted_iota(jnp.int32, (L, L), 0)
    col = lax.broadcasted_iota(jnp.int32, (L, L), 1)

    shifts = [(row - col == s).astype(BF16) for s in range(1, CONV_K)]
    sub8 = lax.broadcasted_iota(jnp.int32, (8, CONV_SLAB), 0)

    def conv_slab(slab):
        cs = slice(slab * CONV_SLAB, (slab + 1) * CONV_SLAB)
        src_ref, lo = (q_ref, slab * CONV_SLAB) if slab * CONV_SLAB < qk_w else (k_ref, slab * CONV_SLAB - qk_w)
        x_b = src_ref[:, lo:lo + CONV_SLAB]
        x_f = x_b.astype(F32)
        tail = tail_scr[:, cs]
        acc = x_f * cw_ref[CONV_K - 1:CONV_K, cs]
        head_fix = jnp.zeros((8, CONV_SLAB), F32)
        for s in range(1, CONV_K):
            w_s = cw_ref[CONV_K - 1 - s:CONV_K - s, cs]
            acc = acc + jnp.dot(shifts[s - 1], x_b, preferred_element_type=F32) * w_s
            head_fix = head_fix + jnp.where(sub8 < s, pltpu.roll(tail, s, 0), 0.0) * w_s
        tail_scr[:, cs] = x_f[L - 8:L, :]
        scale = DQK_A ** -0.5 if slab * CONV_SLAB < qk_w else 1.0
        qk_scr[8:L, cs] = _silu(acc[8:L, :]) * scale
        qk_scr[0:8, cs] = _silu(acc[0:8, :] + head_fix) * scale

    ones_cols = jnp.ones((L, LANES), BF16)

    def head_slices(h):
        return (slice(h * DQK_A, (h + 1) * DQK_A), slice(qk_w + h * DQK_A, qk_w + (h + 1) * DQK_A),
                slice(h * DV_A, (h + 1) * DV_A))

    gc = gc_ref[...] + gbc_ref[...]
    gr = gr_ref[...] + gbr_ref[...]
    causal = col <= row
    tri = causal.astype(F32)
    tri_t = (row <= col).astype(F32)
    b_all = jnp.dot(tri, _log_sigmoid(gc), preferred_element_type=F32, precision=lax.Precision.HIGHEST)
    b_r_all = jnp.dot(_log_sigmoid(gr), tri_t, preferred_element_type=F32, precision=lax.Precision.HIGHEST)
    b_al = pltpu.roll(b_all, LANES - n_heads, 1)
    m_old = m_scr[0:1, :]
    mx = jnp.maximum(_prefix_max_rows(gc - b_al), m_old)
    decay_all = jnp.exp(m_old - mx)
    enm_all = jnp.exp(-(b_al + mx))
    g_tot = b_al[L - 1:L, :]
    w_all = g_tot - b_al + gc
    m_new = jnp.maximum(g_tot + m_old, jnp.max(w_all, axis=0, keepdims=True))
    carry_all = jnp.exp(g_tot + m_old - m_new)
    ew_all = jnp.exp(w_all - m_new)
    m_scr[0:1, :] = m_new

    def stage1(h):
        qs, ks, vs = head_slices(h)
        k_f = qk_scr[:, ks]
        v_ext = jnp.concatenate([v_ref[:, vs], ones_cols], axis=1)
        s_scr[h] = _dot_nt(qk_scr[:, qs].astype(BF16), k_f.astype(BF16))
        c_prev = c_scr[h]
        cb_scr[h] = c_prev.astype(BF16)
        kw = k_f * ew_all[:, h:h + 1]
        c_scr[h] = carry_all[:, h:h + 1] * c_prev + _dot_tn(kw.astype(BF16), v_ext)

    def stage2(h):
        qs, _, vs = head_slices(h)
        v_ext = jnp.concatenate([v_ref[:, vs], ones_cols], axis=1)
        r_row = gr[h:h + 1, :] - b_r_all[n_heads + h:n_heads + h + 1, :]
        p = s_scr[h] * jnp.exp(jnp.where(causal, r_row - mx[:, h:h + 1], NEG_INF))
        q_dec = qk_scr[:, qs] * decay_all[:, h:h + 1]
        lhs = jnp.concatenate([p.astype(BF16), q_dec.astype(BF16)], axis=1)
        rhs = jnp.concatenate([v_ext, cb_scr[h]], axis=0)
        num_den = jnp.dot(lhs, rhs, preferred_element_type=F32)
        den = num_den[:, DV_A:DV_A + LANES]
        inv = 1.0 / jnp.maximum(jnp.abs(den), enm_all[:, h:h + 1])
        hh = num_den[:, 0:DV_A] * jnp.concatenate([inv] * (DV_A // LANES), axis=1)

        ms = jnp.mean(hh * hh, axis=-1, keepdims=True)
        hn = hh * lax.rsqrt(ms + EPS) * ng_ref[0:1, vs]
        gated = _sigmoid(o_ref[:, vs].astype(F32)) * hn * _silu(z_ref[:, vs].astype(F32))
        out_ref[:, vs] = gated.astype(BF16)

    for slab in range(2 * qk_w // CONV_SLAB):
        conv_slab(slab)
    for h in range(n_heads):
        stage1(h)
    for h in range(n_heads):
        stage2(h)


def _mlstm(proj, small, gates_t, conv_w, gb_col, gb_row, norm_g, *, batch, seq, n_heads, offs):
    t = batch * seq
    L = _pick(seq, (MLSTM_CHUNK, 128, 64))
    nc = seq // L
    qk_w = n_heads * DQK_A
    dv_w = n_heads * DV_A
    assert offs["q"] % qk_w == 0 and offs["k"] % qk_w == 0
    assert offs["v"] % dv_w == 0 and offs["o"] % dv_w == 0 and offs["za"] % dv_w == 0
    qb, kb = offs["q"] // qk_w, offs["k"] // qk_w
    vb, ob, zb = offs["v"] // dv_w, offs["o"] // dv_w, offs["za"] // dv_w
    kern = functools.partial(_mlstm_kernel, n_heads=n_heads, chunk=L)
    return pl.pallas_call(
        kern,
        grid=(batch, nc),
        in_specs=[
            pl.BlockSpec((L, qk_w), lambda b, c: (b * nc + c, qb)),
            pl.BlockSpec((L, qk_w), lambda b, c: (b * nc + c, kb)),
            pl.BlockSpec((L, dv_w), lambda b, c: (b * nc + c, vb)),
            pl.BlockSpec((L, dv_w), lambda b, c: (b * nc + c, ob)),
            pl.BlockSpec((L, dv_w), lambda b, c: (b * nc + c, zb)),
            pl.BlockSpec((L, LANES), lambda b, c: (b * nc + c, 1)),
            pl.BlockSpec((16, L), lambda b, c: (0, b * nc + c)),
            pl.BlockSpec((CONV_K, 2 * qk_w), lambda b, c: (0, 0)),
            pl.BlockSpec((1, LANES), lambda b, c: (0, 0)),
            pl.BlockSpec((16, 1), lambda b, c: (0, 0)),
            pl.BlockSpec((1, dv_w), lambda b, c: (0, 0)),
        ],
        out_specs=pl.BlockSpec((L, dv_w), lambda b, c: (b * nc + c, 0)),
        out_shape=jax.ShapeDtypeStruct((t, dv_w), BF16),
        scratch_shapes=[
            pltpu.VMEM((n_heads, DQK_A, DV_A + LANES), F32),
            pltpu.VMEM((8, LANES), F32),
            pltpu.VMEM((L, 2 * qk_w), F32),
            pltpu.VMEM((8, 2 * qk_w), F32),
            pltpu.VMEM((n_heads, L, L), F32),
            pltpu.VMEM((n_heads, DQK_A, DV_A + LANES), BF16),
        ],
        compiler_params=_cparams("parallel", "arbitrary"),
    )(proj, proj, proj, proj, proj, small, gates_t, conv_w, gb_col, gb_row, norm_g)


def _mla_prep_kernel(cq_ref, ckv_ref, kr_ref, cost_ref, sint_ref, cosk_ref, sink_ref, qlg_ref, kvlg_ref,
                     wuqt_ref, wk_ref, wvt_ref, qng_ref, kng_ref, kngs_ref, qt_ref, k_ref, vt_ref, *, n_heads):
    ckv = ckv_ref[...].astype(F32)
    ckvn = (ckv * lax.rsqrt(jnp.mean(ckv * ckv, axis=-1, keepdims=True) + EPS) * kvlg_ref[...]).astype(BF16)
    kn = jnp.dot(ckvn, wk_ref[...], preferred_element_type=F32)
    cq = cq_ref[...].astype(F32)
    cqn = (cq * lax.rsqrt(jnp.mean(cq * cq, axis=-1, keepdims=True) + EPS) * qlg_ref[...]).astype(BF16)
    q_t = _dot_nt(wuqt_ref[...], cqn)
    vt_ref[0] = _dot_nt(wvt_ref[...], ckvn).astype(BF16)

    kr = kr_ref[:, 0:D_ROPE]
    kr_sw = kr_ref[:, D_ROPE:2 * D_ROPE]
    ssr = jnp.sum(kr * kr, axis=-1, keepdims=True)
    k_rot = kr * kng_ref[:, D_NOPE:D_HQK] * cosk_ref[0] + kr_sw * kngs_ref[...] * sink_ref[0]
    for h in range(n_heads):
        kh = kn[:, h * D_NOPE:(h + 1) * D_NOPE]
        ss = jnp.sum(kh * kh, axis=-1, keepdims=True) + ssr
        rs = lax.rsqrt(ss * (1.0 / D_HQK) + EPS)
        k_ref[0, h, :, 0:D_NOPE] = (kh * rs * kng_ref[:, 0:D_NOPE]).astype(BF16)
        k_ref[0, h, :, D_NOPE:D_HQK] = (k_rot * rs).astype(BF16)

    cos_t = cost_ref[0]
    sin_t = sint_ref[0]
    half = D_ROPE // 2
    for h in range(n_heads):
        qh = q_t[h * D_HQK:(h + 1) * D_HQK, :]
        ss = jnp.sum(qh * qh, axis=0, keepdims=True)
        rs = lax.rsqrt(ss * (1.0 / D_HQK) + EPS) * (D_HQK ** -0.5 * LOG2_E)
        qn = qh * rs * qng_ref[...]
        x1 = qn[D_NOPE:D_NOPE + half, :]
        x2 = qn[D_NOPE + half:D_HQK, :]
        qt_ref[0, h, 0:D_NOPE, :] = qn[0:D_NOPE, :].astype(BF16)
        qt_ref[0, h, D_NOPE:D_NOPE + half, :] = (x1 * cos_t - x2 * sin_t).astype(BF16)
        qt_ref[0, h, D_NOPE + half:D_HQK, :] = (x2 * cos_t + x1 * sin_t).astype(BF16)


def _mla_prep(proj3, small3, cos_t, sin_t, cos_k, sin_k, q_lat_g, kv_lat_g, wuq_t, wk, wv_t, qn_g, kn_g, kn_g_sw,
              *, n_heads, offs, layer):
    batch, seq, _ = proj3.shape
    ql, kvl = wuq_t.shape[-1], wk.shape[-2]
    ts = _pick(seq, (PREP_ROWS, 128))
    assert offs["cq"] % ql == 0 and offs["ckv"] % kvl == 0
    cqb, ckvb = offs["cq"] // ql, offs["ckv"] // kvl
    half = D_ROPE // 2
    kern = functools.partial(_mla_prep_kernel, n_heads=n_heads)
    const = lambda b, s: (0, 0)
    return pl.pallas_call(
        kern,
        grid=(batch, seq // ts),
        in_specs=[
            pl.BlockSpec((None, ts, ql), lambda b, s: (b, s, cqb)),
            pl.BlockSpec((None, ts, kvl), lambda b, s: (b, s, ckvb)),
            pl.BlockSpec((None, ts, LANES), lambda b, s: (b, s, 0)),
            pl.BlockSpec((1, half, ts), lambda b, s: (b, 0, s)),
            pl.BlockSpec((1, half, ts), lambda b, s: (b, 0, s)),
            pl.BlockSpec((1, ts, D_ROPE), lambda b, s: (b, s, 0)),
            pl.BlockSpec((1, ts, D_ROPE), lambda b, s: (b, s, 0)),
            pl.BlockSpec((1, ql), const),
            pl.BlockSpec((1, kvl), const),
            pl.BlockSpec((None, n_heads * D_HQK, ql), lambda b, s: (layer, 0, 0)),
            pl.BlockSpec((None, kvl, n_heads * D_NOPE), lambda b, s: (layer, 0, 0)),
            pl.BlockSpec((None, n_heads * D_VB, kvl), lambda b, s: (layer, 0, 0)),
            pl.BlockSpec((D_HQK, 1), const),
            pl.BlockSpec((1, D_HQK), const),
            pl.BlockSpec((1, D_ROPE), const),
        ],
        out_specs=[
            pl.BlockSpec((1, n_heads, D_HQK, ts), lambda b, s: (b, 0, 0, s)),
            pl.BlockSpec((1, n_heads, ts, D_HQK), lambda b, s: (b, 0, s, 0)),
            pl.BlockSpec((1, n_heads * D_VB, ts), lambda b, s: (b, 0, s)),
        ],
        out_shape=[
            jax.ShapeDtypeStruct((batch, n_heads, D_HQK, seq), BF16),
            jax.ShapeDtypeStruct((batch, n_heads, seq, D_HQK), BF16),
            jax.ShapeDtypeStruct((batch, n_heads * D_VB, seq), BF16),
        ],
        compiler_params=_cparams("parallel", "parallel"),
    )(proj3, proj3, small3, cos_t, sin_t, cos_k, sin_k, q_lat_g, kv_lat_g, wuq_t, wk, wv_t, qn_g, kn_g, kn_g_sw)


def _attn_kernel(qt_ref, k_ref, vt_ref, z_ref, o_ref, sa_scr, sb_scr, acc_scr, *, qb, kb, nsub):
    qi = pl.program_id(2)
    kpt = nsub * qb // kb
    assert kpt % 2 == 0
    base = qi * kpt
    every = tuple(range(nsub))
    ones_rows = {n: jnp.where(lax.broadcasted_iota(jnp.int32, (ATTN_PAD_ROWS, n), 0) == 0, 1.0, 0.0).astype(BF16)
                 for n in (qb, kb)}

    def visible(c, diag_r):
        if diag_r is None:
            return kb
        lead = c * qb - diag_r * kb
        return min(max(lead + qb, 0), kb)

    def scores(j, dst, c, n_keys):
        start = pl.multiple_of(j * kb, kb)
        k_blk = k_ref[0, 0, pl.ds(start, n_keys), :]
        q_t = qt_ref[0, 0, :, c * qb:(c + 1) * qb]
        dst[c, 0:n_keys, :] = jnp.dot(k_blk, q_t, preferred_element_type=F32)

    def consume(j, src, ms, c, diag_r):
        n_keys = visible(c, diag_r)
        start = pl.multiple_of(j * kb, kb)
        v_ext = jnp.concatenate([vt_ref[0, :, pl.ds(start, n_keys)], ones_rows[n_keys]], axis=0)
        s = src[c, 0:n_keys, :]
        if diag_r is not None and c * qb - diag_r * kb < n_keys - 1:
            kk = lax.broadcasted_iota(jnp.int32, (n_keys, qb), 0)
            qq = lax.broadcasted_iota(jnp.int32, (n_keys, qb), 1)
            s = jnp.where(kk <= qq + (c * qb - diag_r * kb), s, NEG_INF)
        m_new = jnp.maximum(ms[c], jnp.max(s, axis=0, keepdims=True))
        p = jnp.exp2(s - m_new).astype(BF16)
        alpha = jnp.exp2(ms[c] - m_new)
        acc_scr[c] = alpha * acc_scr[c] + jnp.dot(v_ext, p, preferred_element_type=F32)
        return ms[:c] + (m_new,) + ms[c + 1:]

    def phase(j_cur, src, dst, ms, diag_r, next_r):
        for c in every:
            if next_r != -1 and visible(c, next_r) > 0:
                scores(j_cur + 1, dst, c, visible(c, next_r))
            if visible(c, diag_r) > 0:
                ms = consume(j_cur, src, ms, c, diag_r)
        return ms

    acc_scr[...] = jnp.zeros_like(acc_scr)
    ms = tuple(jnp.full((1, qb), NEG_INF, F32) for _ in every)

    for c in every:
        scores(0, sa_scr, c, kb)

    def body(jj, ms):
        j0 = 2 * jj
        ms = phase(j0, sa_scr, sb_scr, ms, None, None)
        return phase(j0 + 1, sb_scr, sa_scr, ms, None, None)

    ms = lax.fori_loop(0, base // 2, body, ms)

    bufs = (sa_scr, sb_scr)
    for r in range(kpt):
        ms = phase(base + r, bufs[r % 2], bufs[(r + 1) % 2], ms, r, r + 1 if r + 1 < kpt else -1)

    for c in every:
        rows = slice(c * qb, (c + 1) * qb)
        o = (acc_scr[c, 0:D_VB, :] / acc_scr[c, D_VB:D_VB + 1, :]).T
        o_ref[0, rows, :] = (o * _silu(z_ref[0, rows, :].astype(F32))).astype(BF16)


def _attention(q_t, k, v_t, proj3, *, offs):
    batch, n_heads, _, seq = q_t.shape
    tq = _pick(seq, (ATTN_TILE, ATTN_TILE // 2, ATTN_TILE // 4, ATTN_TILE // 8))
    qb = min(ATTN_QBLOCK, tq // 4)
    kb, nsub = 2 * qb, tq // qb
    assert offs["zb"] % D_VB == 0
    zb0 = offs["zb"] // D_VB
    kern = functools.partial(_attn_kernel, qb=qb, kb=kb, nsub=nsub)
    return pl.pallas_call(
        kern,
        grid=(batch, n_heads, seq // tq),
        in_specs=[
            pl.BlockSpec((1, 1, D_HQK, tq), lambda b, h, i: (b, h, 0, i)),
            pl.BlockSpec((1, 1, seq, D_HQK), lambda b, h, i: (b, h, 0, 0)),
            pl.BlockSpec((1, D_VB, seq), lambda b, h, i: (b, h, 0)),
            pl.BlockSpec((1, tq, D_VB), lambda b, h, i: (b, i, zb0 + h)),
        ],
        out_specs=pl.BlockSpec((1, tq, D_VB), lambda b, h, i: (b, i, h)),
        out_shape=jax.ShapeDtypeStruct((batch, seq, n_heads * D_VB), BF16),
        scratch_shapes=[
            pltpu.VMEM((nsub, kb, qb), F32),
            pltpu.VMEM((nsub, kb, qb), F32),
            pltpu.VMEM((nsub, D_VB + ATTN_PAD_ROWS, qb), F32),
        ],
        compiler_params=_cparams("parallel", "parallel", "arbitrary"),
    )(q_t, k, v_t, proj3)


def _merge_kernel(ha_ref, ob_ref, wa_ref, wb_ref, ga_ref, gb_ref, y_ref):
    ya = jnp.dot(ha_ref[...], wa_ref[...], preferred_element_type=F32)
    yb = jnp.dot(ob_ref[...], wb_ref[...], preferred_element_type=F32)
    y = _sigmoid(ga_ref[...].astype(F32)) * ya + _sigmoid(gb_ref[...].astype(F32)) * yb
    y_ref[...] = y.astype(BF16)


def _merge(ha, ob, w_a, w_b, proj, *, offs, layer):
    t, da = ha.shape
    db = ob.shape[1]
    d = w_a.shape[-1]
    bm = _pick(t, (1024, 512, 256, 128))
    bn = _pick(d, (1024, 512, 256, 128))
    assert offs["ga"] % bn == 0 and offs["gb"] % bn == 0
    ga0, gb0 = offs["ga"] // bn, offs["gb"] // bn
    return pl.pallas_call(
        _merge_kernel,
        grid=(t // bm, d // bn),
        in_specs=[
            pl.BlockSpec((bm, da), lambda i, j: (i, 0)),
            pl.BlockSpec((bm, db), lambda i, j: (i, 0)),
            pl.BlockSpec((None, da, bn), lambda i, j: (layer, 0, j)),
            pl.BlockSpec((None, db, bn), lambda i, j: (layer, 0, j)),
            pl.BlockSpec((bm, bn), lambda i, j: (i, ga0 + j)),
            pl.BlockSpec((bm, bn), lambda i, j: (i, gb0 + j)),
        ],
        out_specs=pl.BlockSpec((bm, bn), lambda i, j: (i, j)),
        out_shape=jax.ShapeDtypeStruct((t, d), BF16),
        compiler_params=_cparams("parallel", "arbitrary"),
    )(ha, ob, w_a, w_b, proj, proj)


def _outproj_kernel(y_ref, w_ref, x_ref, o_ref):
    o_ref[...] = x_ref[...] + jnp.dot(y_ref[...], w_ref[...], preferred_element_type=F32)


def _outproj(y, w_out, x2, *, layer):
    t, d = y.shape
    bm = _pick(t, (512, 256, 128))
    bn = _pick(d, (2048, 1024, 512, 256, 128))
    return pl.pallas_call(
        _outproj_kernel,
        grid=(t // bm, d // bn),
        in_specs=[
            pl.BlockSpec((bm, d), lambda i, j: (i, 0)),
            pl.BlockSpec((None, d, bn), lambda i, j: (layer, 0, j)),
            pl.BlockSpec((bm, bn), lambda i, j: (i, j)),
        ],
        out_specs=pl.BlockSpec((bm, bn), lambda i, j: (i, j)),
        out_shape=jax.ShapeDtypeStruct((t, d), F32),
        compiler_params=_cparams("parallel", "arbitrary"),
    )(y, w_out, x2)


def kernel(x, positions, norm_g, w_in, gate_bias, conv_w, mlstm_norm_g, w_a, q_lat_g, kv_lat_g, w_uq, w_ukv,
           q_norm_g, k_norm_g, w_b, w_out):
    batch, seq, d = x.shape
    depth = w_in.shape[0]
    h_a = gate_bias.shape[-1] // 2
    h_b = w_uq.shape[-1] // D_HQK
    qk_w, d_a, d_b = h_a * DQK_A, h_a * DV_A, h_b * D_VB
    ql, kvl = w_uq.shape[1], w_ukv.shape[1]
    assert 2 * h_a <= 16 and w_ukv.shape[-1] == h_b * (D_NOPE + D_VB)

    sizes = (qk_w, qk_w, d_a, d_a, h_a, h_a, d_a, ql, kvl, D_ROPE, d_b, d, d)
    names = ("q", "k", "v", "o", "i", "f", "za", "cq", "ckv", "kr", "zb", "ga", "gb")
    src, acc = {}, 0
    for nme, sz in zip(names, sizes):
        src[nme] = (acc, acc + sz)
        acc += sz
    assert acc == w_in.shape[-1]

    order = ("q", "k", "v", "o", "za", "cq", "ckv", "zb", "ga", "gb")
    offs, acc = {}, 0
    for nme in order:
        offs[nme] = acc
        acc += src[nme][1] - src[nme][0]
    runs = [(src[first][0], src[last][1]) for first, last in (("q", "o"), ("za", "ckv"), ("zb", "gb"))]
    bn = next(c for c in (1536, 1024, 512, 256, 128) if all((hi - lo) % c == 0 for lo, hi in runs))
    tile_rows = []
    for lo, hi in runs:
        assert lo % ROW_ALIGN == 0 and bn % ROW_ALIGN == 0
        tile_rows += [r // ROW_ALIGN for r in range(lo, hi, bn)]
    tile_rows = jnp.asarray(tile_rows, jnp.int32)

    w_t = jnp.swapaxes(w_in, 1, 2).astype(BF16)
    half = D_ROPE // 2
    kr0 = src["kr"][0]
    w_small_t = jnp.concatenate([
        w_t[:, kr0:kr0 + D_ROPE], w_t[:, kr0 + half:kr0 + D_ROPE], w_t[:, kr0:kr0 + half],
        w_t[:, src["i"][0]:src["f"][1]],
        jnp.zeros((depth, LANES - 2 * h_a, d), BF16)], axis=1)

    gb_col = jnp.pad(gate_bias, ((0, 0), (0, LANES - 2 * h_a)))[:, None, :]
    gb_row = jnp.pad(gate_bias, ((0, 0), (0, 16 - 2 * h_a)))[:, :, None]

    w_a_b, w_b_b, w_out_b = w_a.astype(BF16), w_b.astype(BF16), w_out.astype(BF16)
    wuq_t = jnp.swapaxes(w_uq, 1, 2).astype(BF16)
    w_ukv4 = w_ukv.reshape(depth, kvl, h_b, D_NOPE + D_VB)
    wk = w_ukv4[..., :D_NOPE].reshape(depth, kvl, h_b * D_NOPE).astype(BF16)
    wv_t = jnp.swapaxes(w_ukv4[..., D_NOPE:].reshape(depth, kvl, h_b * D_VB), 1, 2).astype(BF16)
    qn_g = q_norm_g[:, :, None]
    kn_g = k_norm_g[:, None, :]
    kn_g_sw = jnp.concatenate([k_norm_g[:, D_NOPE + half:], k_norm_g[:, D_NOPE:D_NOPE + half]], axis=-1)[:, None, :]

    inv_freq = jnp.exp(-math.log(ROPE_THETA) * jnp.arange(0, D_ROPE, 2, dtype=F32) / D_ROPE)
    ang = positions.astype(F32)[..., None] * inv_freq
    cos, sin = jnp.cos(ang), jnp.sin(ang)
    cos_t, sin_t = jnp.swapaxes(cos, 1, 2), jnp.swapaxes(sin, 1, 2)
    cos_k = jnp.concatenate([cos, cos], axis=-1)
    sin_k = jnp.concatenate([-sin, sin], axis=-1)

    x2 = x.reshape(batch * seq, d)
    for l in range(depth):
        proj, small, gates_t = _inproj(x2, norm_g[l][None, :], w_t, tile_rows, w_small_t, layer=l, bn=bn)
        h_gated = _mlstm(proj, small, gates_t, conv_w[l], gb_col[l], gb_row[l], mlstm_norm_g[l][None, :],
                         batch=batch, seq=seq, n_heads=h_a, offs=offs)
        proj3 = proj.reshape(batch, seq, -1)
        small3 = small.reshape(batch, seq, -1)
        q_t, k, v_t = _mla_prep(proj3, small3, cos_t, sin_t, cos_k, sin_k, q_lat_g[l][None, :], kv_lat_g[l][None, :],
                                wuq_t, wk, wv_t, qn_g[l], kn_g[l], kn_g_sw[l], n_heads=h_b, offs=offs, layer=l)
        o_gated = _attention(q_t, k, v_t, proj3, offs=offs).reshape(batch * seq, d_b)
        y = _merge(h_gated, o_gated, w_a_b, w_b_b, proj, offs=offs, layer=l)
        x2 = _outproj(y, w_out_b, x2, layer=l)
    return x2.reshape(batch, seq, d)
```

```python
import functools
import math

import jax
import jax.numpy as jnp
from jax import lax
from jax.experimental import pallas as pl
from jax.experimental.pallas import tpu as pltpu

F32 = jnp.float32
BF16 = jnp.bfloat16

DQK_A = 128
DV_A = 256
CONV_K = 4
D_NOPE = 128
D_ROPE = 64
D_HQK = D_NOPE + D_ROPE
D_VB = 128
ROPE_THETA = 10000.0
EPS = 1e-6
NEG_INF = -1e30
LOG2_E = 1.4426950408889634

LANES = 128
ROW_ALIGN = 16
VMEM_LIMIT_BYTES = 56 * 1024 * 1024

MLSTM_CHUNK = 256
CONV_SLAB = 256
ATTN_TILE = 4096
ATTN_QBLOCK = 256
ATTN_PAD_ROWS = 16
PREP_ROWS = 256
NORM_ROWS = 256


def _pick(n, candidates):
    for c in candidates:
        if n % c == 0:
            return c
    raise ValueError(f"no tile in {candidates} divides {n}")


def _cparams(*sem):
    return pltpu.CompilerParams(dimension_semantics=sem, vmem_limit_bytes=VMEM_LIMIT_BYTES)


def _sigmoid(x):
    return 1.0 / (1.0 + jnp.exp2(x * (-LOG2_E)))


def _silu(x):
    return x * _sigmoid(x)


def _log_sigmoid(x):
    return jnp.minimum(x, 0.0) - jnp.log(1.0 + jnp.exp(-jnp.abs(x)))


def _prefix_max_rows(x):
    n = x.shape[0]
    rows = lax.broadcasted_iota(jnp.int32, x.shape, 0)
    k = 1
    while k < n:
        if k < 8:
            shifted = jnp.where(rows < k, NEG_INF, pltpu.roll(x, k, 0))
        else:
            shifted = jnp.concatenate([jnp.full((k, x.shape[1]), NEG_INF, x.dtype), x[:n - k, :]], axis=0)
        x = jnp.maximum(x, shifted)
        k *= 2
    return x


def _dot_nt(a, b):
    return lax.dot_general(a, b, (((1,), (1,)), ((), ())), preferred_element_type=F32)


def _dot_tn(a, b):
    return lax.dot_general(a, b, (((0,), (0,)), ((), ())), preferred_element_type=F32)


def _inproj_kernel(rows_ref, x_ref, g_ref, wt_ref, wst_ref, o_ref, os_ref, h_scr):
    del rows_ref
    @pl.when(pl.program_id(1) == 0)
    def _():
        rows = x_ref.shape[0]
        step = min(NORM_ROWS, rows)
        for r0 in range(0, rows, step):
            x = x_ref[r0:r0 + step, :]
            ms = jnp.mean(x * x, axis=-1, keepdims=True)
            h_scr[r0:r0 + step, :] = (x * lax.rsqrt(ms + EPS) * g_ref[...]).astype(BF16)
        os_ref[...] = _dot_nt(h_scr[...], wst_ref[...])

    o_ref[...] = _dot_nt(h_scr[...], wt_ref[0]).astype(BF16)


def _inproj(x2, norm_g, w_t, tile_rows, w_small_t, *, layer, bn):
    t, d = x2.shape
    n_tiles = tile_rows.shape[0]
    nm = n_tiles * bn
    bm = _pick(t, (1024, 512, 256, 128))
    grid_spec = pltpu.PrefetchScalarGridSpec(
        num_scalar_prefetch=1,
        grid=(t // bm, n_tiles),
        in_specs=[
            pl.BlockSpec((bm, d), lambda i, j, rows: (i, 0)),
            pl.BlockSpec((1, d), lambda i, j, rows: (0, 0)),
            pl.BlockSpec((pl.Element(1), pl.Element(bn), pl.Element(d)),
                         lambda i, j, rows: (layer, rows[j] * ROW_ALIGN, 0)),
            pl.BlockSpec((None, 2 * LANES, d), lambda i, j, rows: (layer, 0, 0)),
        ],
        out_specs=[
            pl.BlockSpec((bm, bn), lambda i, j, rows: (i, j)),
            pl.BlockSpec((bm, 2 * LANES), lambda i, j, rows: (i, 0)),
        ],
        scratch_shapes=[pltpu.VMEM((bm, d), BF16)],
    )
    return pl.pallas_call(
        _inproj_kernel,
        grid_spec=grid_spec,
        out_shape=[
            jax.ShapeDtypeStruct((t, nm), BF16),
            jax.ShapeDtypeStruct((t, 2 * LANES), F32),
        ],
        compiler_params=_cparams("parallel", "arbitrary"),
    )(tile_rows, x2, norm_g, w_t, w_small_t)


def _mlstm_kernel(q_ref, k_ref, v_ref, o_ref, z_ref, gc_ref, cw_ref, gbc_ref, ng_ref,
                  out_ref, c_scr, m_scr, qk_scr, tail_scr, s_scr, cb_scr, *, n_heads, chunk):
    L = chunk
    qk_w = n_heads * DQK_A
    c_idx = pl.program_id(1)

    @pl.when(c_idx == 0)
    def _():
        c_scr[...] = jnp.zeros_like(c_scr)
        m_scr[...] = jnp.zeros_like(m_scr)
        tail_scr[...] = jnp.zeros_like(tail_scr)

    row = lax.broadcasted_iota(jnp.int32, (L, L), 0)
    col = lax.broadcasted_iota(jnp.int32, (L, L), 1)

    shifts = [(row - col == s).astype(BF16) for s in range(1, CONV_K)]
    sub8 = lax.broadcasted_iota(jnp.int32, (8, CONV_SLAB), 0)

    def conv_slab(slab):
        cs = slice(slab * CONV_SLAB, (slab + 1) * CONV_SLAB)
        src_ref, lo = (q_ref, slab * CONV_SLAB) if slab * CONV_SLAB < qk_w else (k_ref, slab * CONV_SLAB - qk_w)
        x_b = src_ref[:, lo:lo + CONV_SLAB]
        x_f = x_b.astype(F32)
        tail = tail_scr[:, cs]
        acc = x_f * cw_ref[CONV_K - 1:CONV_K, cs]
        head_fix = jnp.zeros((8, CONV_SLAB), F32)
        for s in range(1, CONV_K):
            w_s = cw_ref[CONV_K - 1 - s:CONV_K - s, cs]
            acc = acc + jnp.dot(shifts[s - 1], x_b, preferred_element_type=F32) * w_s
            head_fix = head_fix + jnp.where(sub8 < s, pltpu.roll(tail, s, 0), 0.0) * w_s
        tail_scr[:, cs] = x_f[L - 8:L, :]
        scale = DQK_A ** -0.5 if slab * CONV_SLAB < qk_w else 1.0
        qk_scr[8:L, cs] = _silu(acc[8:L, :]) * scale
        qk_scr[0:8, cs] = _silu(acc[0:8, :] + head_fix) * scale

    ones_cols = jnp.ones((L, LANES), BF16)

    def head_slices(h):
        return (slice(h * DQK_A, (h + 1) * DQK_A), slice(qk_w + h * DQK_A, qk_w + (h + 1) * DQK_A),
                slice(h * DV_A, (h + 1) * DV_A))

    gc = gc_ref[...] + gbc_ref[...]
    causal = col <= row
    tri = causal.astype(F32)
    b_all = jnp.dot(tri, _log_sigmoid(gc), preferred_element_type=F32, precision=lax.Precision.HIGHEST)
    b_al = pltpu.roll(b_all, LANES - n_heads, 1)
    m_old = m_scr[0:1, :]
    r_col = gc - b_al
    r_rows = r_col.T
    mx = jnp.maximum(_prefix_max_rows(r_col), m_old)
    decay_all = jnp.exp(m_old - mx)
    enm_all = jnp.exp(-(b_al + mx))
    g_tot = b_al[L - 1:L, :]
    w_all = g_tot - b_al + gc
    m_new = jnp.maximum(g_tot + m_old, jnp.max(w_all, axis=0, keepdims=True))
    carry_all = jnp.exp(g_tot + m_old - m_new)
    ew_all = jnp.exp(w_all - m_new)
    m_scr[0:1, :] = m_new

    def stage1(h):
        qs, ks, vs = head_slices(h)
        k_f = qk_scr[:, ks]
        v_ext = jnp.concatenate([v_ref[:, vs], ones_cols], axis=1)
        s_scr[h] = _dot_nt(qk_scr[:, qs].astype(BF16), k_f.astype(BF16))
        c_prev = c_scr[h]
        cb_scr[h] = c_prev.astype(BF16)
        kw = k_f * ew_all[:, h:h + 1]
        c_scr[h] = carry_all[:, h:h + 1] * c_prev + _dot_tn(kw.astype(BF16), v_ext)

    def stage2(h):
        qs, _, vs = head_slices(h)
        v_ext = jnp.concatenate([v_ref[:, vs], ones_cols], axis=1)
        p = s_scr[h] * jnp.exp(jnp.where(causal, r_rows[h:h + 1, :] - mx[:, h:h + 1], NEG_INF))
        q_dec = qk_scr[:, qs] * decay_all[:, h:h + 1]
        lhs = jnp.concatenate([p.astype(BF16), q_dec.astype(BF16)], axis=1)
        rhs = jnp.concatenate([v_ext, cb_scr[h]], axis=0)
        num_den = jnp.dot(lhs, rhs, preferred_element_type=F32)
        den = num_den[:, DV_A:DV_A + LANES]
        inv = 1.0 / jnp.maximum(jnp.abs(den), enm_all[:, h:h + 1])
        hh = num_den[:, 0:DV_A] * jnp.concatenate([inv] * (DV_A // LANES), axis=1)

        ms = jnp.mean(hh * hh, axis=-1, keepdims=True)
        hn = hh * lax.rsqrt(ms + EPS) * ng_ref[0:1, vs]
        gated = _sigmoid(o_ref[:, vs].astype(F32)) * hn * _silu(z_ref[:, vs].astype(F32))
        out_ref[:, vs] = gated.astype(BF16)

    for slab in range(2 * qk_w // CONV_SLAB):
        conv_slab(slab)
    for h in range(n_heads):
        stage1(h)
    for h in range(n_heads):
        stage2(h)


def _mlstm(proj, small, conv_w, gb_col, norm_g, *, batch, seq, n_heads, offs):
    t = batch * seq
    L = _pick(seq, (MLSTM_CHUNK, 128, 64))
    nc = seq // L
    qk_w = n_heads * DQK_A
    dv_w = n_heads * DV_A
    assert offs["q"] % qk_w == 0 and offs["k"] % qk_w == 0
    assert offs["v"] % dv_w == 0 and offs["o"] % dv_w == 0 and offs["za"] % dv_w == 0
    qb, kb = offs["q"] // qk_w, offs["k"] // qk_w
    vb, ob, zb = offs["v"] // dv_w, offs["o"] // dv_w, offs["za"] // dv_w
    kern = functools.partial(_mlstm_kernel, n_heads=n_heads, chunk=L)
    return pl.pallas_call(
        kern,
        grid=(batch, nc),
        in_specs=[
            pl.BlockSpec((L, qk_w), lambda b, c: (b * nc + c, qb)),
            pl.BlockSpec((L, qk_w), lambda b, c: (b * nc + c, kb)),
            pl.BlockSpec((L, dv_w), lambda b, c: (b * nc + c, vb)),
            pl.BlockSpec((L, dv_w), lambda b, c: (b * nc + c, ob)),
            pl.BlockSpec((L, dv_w), lambda b, c: (b * nc + c, zb)),
            pl.BlockSpec((L, LANES), lambda b, c: (b * nc + c, 1)),
            pl.BlockSpec((CONV_K, 2 * qk_w), lambda b, c: (0, 0)),
            pl.BlockSpec((1, LANES), lambda b, c: (0, 0)),
            pl.BlockSpec((1, dv_w), lambda b, c: (0, 0)),
        ],
        out_specs=pl.BlockSpec((L, dv_w), lambda b, c: (b * nc + c, 0)),
        out_shape=jax.ShapeDtypeStruct((t, dv_w), BF16),
        scratch_shapes=[
            pltpu.VMEM((n_heads, DQK_A, DV_A + LANES), F32),
            pltpu.VMEM((8, LANES), F32),
            pltpu.VMEM((L, 2 * qk_w), F32),
            pltpu.VMEM((8, 2 * qk_w), F32),
            pltpu.VMEM((n_heads, L, L), F32),
            pltpu.VMEM((n_heads, DQK_A, DV_A + LANES), BF16),
        ],
        compiler_params=_cparams("parallel", "arbitrary"),
    )(proj, proj, proj, proj, proj, small, conv_w, gb_col, norm_g)


def _mla_prep_kernel(cq_ref, ckv_ref, kr_ref, cost_ref, sint_ref, cosk_ref, sink_ref, qlg_ref, kvlg_ref,
                     wuqt_ref, wk_ref, wvt_ref, qng_ref, kng_ref, kngs_ref, qt_ref, k_ref, vt_ref, *, n_heads):
    ckv = ckv_ref[...].astype(F32)
    ckvn = (ckv * lax.rsqrt(jnp.mean(ckv * ckv, axis=-1, keepdims=True) + EPS) * kvlg_ref[...]).astype(BF16)
    kn = jnp.dot(ckvn, wk_ref[...], preferred_element_type=F32)
    cq = cq_ref[...].astype(F32)
    cqn = (cq * lax.rsqrt(jnp.mean(cq * cq, axis=-1, keepdims=True) + EPS) * qlg_ref[...]).astype(BF16)
    q_t = _dot_nt(wuqt_ref[...], cqn)
    vt_ref[0] = _dot_nt(wvt_ref[...], ckvn).astype(BF16)

    kr = kr_ref[:, 0:D_ROPE]
    kr_sw = kr_ref[:, D_ROPE:2 * D_ROPE]
    ssr = jnp.sum(kr * kr, axis=-1, keepdims=True)
    k_rot = kr * kng_ref[:, D_NOPE:D_HQK] * cosk_ref[0] + kr_sw * kngs_ref[...] * sink_ref[0]
    for h in range(n_heads):
        kh = kn[:, h * D_NOPE:(h + 1) * D_NOPE]
        ss = jnp.sum(kh * kh, axis=-1, keepdims=True) + ssr
        rs = lax.rsqrt(ss * (1.0 / D_HQK) + EPS)
        k_ref[0, h, :, 0:D_NOPE] = (kh * rs * kng_ref[:, 0:D_NOPE]).astype(BF16)
        k_ref[0, h, :, D_NOPE:D_HQK] = (k_rot * rs).astype(BF16)

    cos_t = cost_ref[0]
    sin_t = sint_ref[0]
    half = D_ROPE // 2
    for h in range(n_heads):
        qh = q_t[h * D_HQK:(h + 1) * D_HQK, :]
        ss = jnp.sum(qh * qh, axis=0, keepdims=True)
        rs = lax.rsqrt(ss * (1.0 / D_HQK) + EPS) * (D_HQK ** -0.5 * LOG2_E)
        qn = qh * rs * qng_ref[...]
        x1 = qn[D_NOPE:D_NOPE + half, :]
        x2 = qn[D_NOPE + half:D_HQK, :]
        qt_ref[0, h, 0:D_NOPE, :] = qn[0:D_NOPE, :].astype(BF16)
        qt_ref[0, h, D_NOPE:D_NOPE + half, :] = (x1 * cos_t - x2 * sin_t).astype(BF16)
        qt_ref[0, h, D_NOPE + half:D_HQK, :] = (x2 * cos_t + x1 * sin_t).astype(BF16)


def _mla_prep(proj3, small3, cos_t, sin_t, cos_k, sin_k, q_lat_g, kv_lat_g, wuq_t, wk, wv_t, qn_g, kn_g, kn_g_sw,
              *, n_heads, offs, layer):
    batch, seq, _ = proj3.shape
    ql, kvl = wuq_t.shape[-1], wk.shape[-2]
    ts = _pick(seq, (PREP_ROWS, 128))
    assert offs["cq"] % ql == 0 and offs["ckv"] % kvl == 0
    cqb, ckvb = offs["cq"] // ql, offs["ckv"] // kvl
    half = D_ROPE // 2
    kern = functools.partial(_mla_prep_kernel, n_heads=n_heads)
    const = lambda b, s: (0, 0)
    return pl.pallas_call(
        kern,
        grid=(batch, seq // ts),
        in_specs=[
            pl.BlockSpec((None, ts, ql), lambda b, s: (b, s, cqb)),
            pl.BlockSpec((None, ts, kvl), lambda b, s: (b, s, ckvb)),
            pl.BlockSpec((None, ts, LANES), lambda b, s: (b, s, 0)),
            pl.BlockSpec((1, half, ts), lambda b, s: (b, 0, s)),
            pl.BlockSpec((1, half, ts), lambda b, s: (b, 0, s)),
            pl.BlockSpec((1, ts, D_ROPE), lambda b, s: (b, s, 0)),
            pl.BlockSpec((1, ts, D_ROPE), lambda b, s: (b, s, 0)),
            pl.BlockSpec((1, ql), const),
            pl.BlockSpec((1, kvl), const),
            pl.BlockSpec((None, n_heads * D_HQK, ql), lambda b, s: (layer, 0, 0)),
            pl.BlockSpec((None, kvl, n_heads * D_NOPE), lambda b, s: (layer, 0, 0)),
            pl.BlockSpec((None, n_heads * D_VB, kvl), lambda b, s: (layer, 0, 0)),
            pl.BlockSpec((D_HQK, 1), const),
            pl.BlockSpec((1, D_HQK), const),
            pl.BlockSpec((1, D_ROPE), const),
        ],
        out_specs=[
            pl.BlockSpec((1, n_heads, D_HQK, ts), lambda b, s: (b, 0, 0, s)),
            pl.BlockSpec((1, n_heads, ts, D_HQK), lambda b, s: (b, 0, s, 0)),
            pl.BlockSpec((1, n_heads * D_VB, ts), lambda b, s: (b, 0, s)),
        ],
        out_shape=[
            jax.ShapeDtypeStruct((batch, n_heads, D_HQK, seq), BF16),
            jax.ShapeDtypeStruct((batch, n_heads, seq, D_HQK), BF16),
            jax.ShapeDtypeStruct((batch, n_heads * D_VB, seq), BF16),
        ],
        compiler_params=_cparams("parallel", "parallel"),
    )(proj3, proj3, small3, cos_t, sin_t, cos_k, sin_k, q_lat_g, kv_lat_g, wuq_t, wk, wv_t, qn_g, kn_g, kn_g_sw)


def _attn_kernel(qt_ref, k_ref, vt_ref, z_ref, o_ref, sa_scr, sb_scr, acc_scr, *, qb, kb, nsub):
    qi = pl.program_id(2)
    kpt = nsub * qb // kb
    assert kpt % 2 == 0
    base = qi * kpt
    every = tuple(range(nsub))
    ones_rows = {n: jnp.where(lax.broadcasted_iota(jnp.int32, (ATTN_PAD_ROWS, n), 0) == 0, 1.0, 0.0).astype(BF16)
                 for n in (qb, kb)}

    def visible(c, diag_r):
        if diag_r is None:
            return kb
        lead = c * qb - diag_r * kb
        return min(max(lead + qb, 0), kb)

    def scores(j, dst, c, n_keys):
        start = pl.multiple_of(j * kb, kb)
        k_blk = k_ref[0, 0, pl.ds(start, n_keys), :]
        q_t = qt_ref[0, 0, :, c * qb:(c + 1) * qb]
        dst[c, 0:n_keys, :] = jnp.dot(k_blk, q_t, preferred_element_type=F32)

    def consume(j, src, ms, c, diag_r):
        n_keys = visible(c, diag_r)
        start = pl.multiple_of(j * kb, kb)
        v_ext = jnp.concatenate([vt_ref[0, :, pl.ds(start, n_keys)], ones_rows[n_keys]], axis=0)
        s = src[c, 0:n_keys, :]
        if diag_r is not None and c * qb - diag_r * kb < n_keys - 1:
            kk = lax.broadcasted_iota(jnp.int32, (n_keys, qb), 0)
            qq = lax.broadcasted_iota(jnp.int32, (n_keys, qb), 1)
            s = jnp.where(kk <= qq + (c * qb - diag_r * kb), s, NEG_INF)
        m_new = jnp.maximum(ms[c], jnp.max(s, axis=0, keepdims=True))
        p = jnp.exp2(s - m_new).astype(BF16)
        alpha = jnp.exp2(ms[c] - m_new)
        acc_scr[c] = alpha * acc_scr[c] + jnp.dot(v_ext, p, preferred_element_type=F32)
        return ms[:c] + (m_new,) + ms[c + 1:]

    def phase(j_cur, src, dst, ms, diag_r, next_r):
        for c in every:
            if next_r != -1 and visible(c, next_r) > 0:
                scores(j_cur + 1, dst, c, visible(c, next_r))
            if visible(c, diag_r) > 0:
                ms = consume(j_cur, src, ms, c, diag_r)
        return ms

    acc_scr[...] = jnp.zeros_like(acc_scr)
    ms = tuple(jnp.full((1, qb), NEG_INF, F32) for _ in every)

    for c in every:
        scores(0, sa_scr, c, kb)

    def body(jj, ms):
        j0 = 2 * jj
        ms = phase(j0, sa_scr, sb_scr, ms, None, None)
        return phase(j0 + 1, sb_scr, sa_scr, ms, None, None)

    ms = lax.fori_loop(0, base // 2, body, ms)

    bufs = (sa_scr, sb_scr)
    for r in range(kpt):
        ms = phase(base + r, bufs[r % 2], bufs[(r + 1) % 2], ms, r, r + 1 if r + 1 < kpt else -1)

    for c in every:
        rows = slice(c * qb, (c + 1) * qb)
        o = (acc_scr[c, 0:D_VB, :] / acc_scr[c, D_VB:D_VB + 1, :]).T
        o_ref[0, rows, :] = (o * _silu(z_ref[0, rows, :].astype(F32))).astype(BF16)


def _attention(q_t, k, v_t, proj3, *, offs):
    batch, n_heads, _, seq = q_t.shape
    tq = _pick(seq, (ATTN_TILE, ATTN_TILE // 2, ATTN_TILE // 4, ATTN_TILE // 8))
    qb = min(ATTN_QBLOCK, tq // 4)
    kb, nsub = 2 * qb, tq // qb
    assert offs["zb"] % D_VB == 0
    zb0 = offs["zb"] // D_VB
    kern = functools.partial(_attn_kernel, qb=qb, kb=kb, nsub=nsub)
    return pl.pallas_call(
        kern,
        grid=(batch, n_heads, seq // tq),
        in_specs=[
            pl.BlockSpec((1, 1, D_HQK, tq), lambda b, h, i: (b, h, 0, i)),
            pl.BlockSpec((1, 1, seq, D_HQK), lambda b, h, i: (b, h, 0, 0)),
            pl.BlockSpec((1, D_VB, seq), lambda b, h, i: (b, h, 0)),
            pl.BlockSpec((1, tq, D_VB), lambda b, h, i: (b, i, zb0 + h)),
        ],
        out_specs=pl.BlockSpec((1, tq, D_VB), lambda b, h, i: (b, i, h)),
        out_shape=jax.ShapeDtypeStruct((batch, seq, n_heads * D_VB), BF16),
        scratch_shapes=[
            pltpu.VMEM((nsub, kb, qb), F32),
            pltpu.VMEM((nsub, kb, qb), F32),
            pltpu.VMEM((nsub, D_VB + ATTN_PAD_ROWS, qb), F32),
        ],
        compiler_params=_cparams("parallel", "parallel", "arbitrary"),
    )(q_t, k, v_t, proj3)


def _merge_kernel(ha_ref, ob_ref, wa_ref, wb_ref, ga_ref, gb_ref, y_ref):
    ya = jnp.dot(ha_ref[...], wa_ref[...], preferred_element_type=F32)
    yb = jnp.dot(ob_ref[...], wb_ref[...], preferred_element_type=F32)
    y = _sigmoid(ga_ref[...].astype(F32)) * ya + _sigmoid(gb_ref[...].astype(F32)) * yb
    y_ref[...] = y.astype(BF16)


def _merge(ha, ob, w_a, w_b, proj, *, offs, layer):
    t, da = ha.shape
    db = ob.shape[1]
    d = w_a.shape[-1]
    bm = _pick(t, (1024, 512, 256, 128))
    bn = _pick(d, (1024, 512, 256, 128))
    assert offs["ga"] % bn == 0 and offs["gb"] % bn == 0
    ga0, gb0 = offs["ga"] // bn, offs["gb"] // bn
    return pl.pallas_call(
        _merge_kernel,
        grid=(t // bm, d // bn),
        in_specs=[
            pl.BlockSpec((bm, da), lambda i, j: (i, 0)),
            pl.BlockSpec((bm, db), lambda i, j: (i, 0)),
            pl.BlockSpec((None, da, bn), lambda i, j: (layer, 0, j)),
            pl.BlockSpec((None, db, bn), lambda i, j: (layer, 0, j)),
            pl.BlockSpec((bm, bn), lambda i, j: (i, ga0 + j)),
            pl.BlockSpec((bm, bn), lambda i, j: (i, gb0 + j)),
        ],
        out_specs=pl.BlockSpec((bm, bn), lambda i, j: (i, j)),
        out_shape=jax.ShapeDtypeStruct((t, d), BF16),
        compiler_params=_cparams("parallel", "arbitrary"),
    )(ha, ob, w_a, w_b, proj, proj)


def _outproj_kernel(y_ref, w_ref, x_ref, o_ref):
    o_ref[...] = x_ref[...] + jnp.dot(y_ref[...], w_ref[...], preferred_element_type=F32)


def _outproj(y, w_out, x2, *, layer):
    t, d = y.shape
    bm = _pick(t, (512, 256, 128))
    bn = _pick(d, (2048, 1024, 512, 256, 128))
    return pl.pallas_call(
        _outproj_kernel,
        grid=(t // bm, d // bn),
        in_specs=[
            pl.BlockSpec((bm, d), lambda i, j: (i, 0)),
            pl.BlockSpec((None, d, bn), lambda i, j: (layer, 0, j)),
            pl.BlockSpec((bm, bn), lambda i, j: (i, j)),
        ],
        out_specs=pl.BlockSpec((bm, bn), lambda i, j: (i, j)),
        out_shape=jax.ShapeDtypeStruct((t, d), F32),
        compiler_params=_cparams("parallel", "arbitrary"),
    )(y, w_out, x2)


def kernel(x, positions, norm_g, w_in, gate_bias, conv_w, mlstm_norm_g, w_a, q_lat_g, kv_lat_g, w_uq, w_ukv,
           q_norm_g, k_norm_g, w_b, w_out):
    batch, seq, d = x.shape
    depth = w_in.shape[0]
    h_a = gate_bias.shape[-1] // 2
    h_b = w_uq.shape[-1] // D_HQK
    qk_w, d_a, d_b = h_a * DQK_A, h_a * DV_A, h_b * D_VB
    ql, kvl = w_uq.shape[1], w_ukv.shape[1]
    assert 2 * h_a <= 16 and w_ukv.shape[-1] == h_b * (D_NOPE + D_VB)

    sizes = (qk_w, qk_w, d_a, d_a, h_a, h_a, d_a, ql, kvl, D_ROPE, d_b, d, d)
    names = ("q", "k", "v", "o", "i", "f", "za", "cq", "ckv", "kr", "zb", "ga", "gb")
    src, acc = {}, 0
    for nme, sz in zip(names, sizes):
        src[nme] = (acc, acc + sz)
        acc += sz
    assert acc == w_in.shape[-1]

    order = ("q", "k", "v", "o", "za", "cq", "ckv", "zb", "ga", "gb")
    offs, acc = {}, 0
    for nme in order:
        offs[nme] = acc
        acc += src[nme][1] - src[nme][0]
    runs = [(src[first][0], src[last][1]) for first, last in (("q", "o"), ("za", "ckv"), ("zb", "gb"))]
    bn = next(c for c in (1536, 1024, 512, 256, 128) if all((hi - lo) % c == 0 for lo, hi in runs))
    tile_rows = []
    for lo, hi in runs:
        assert lo % ROW_ALIGN == 0 and bn % ROW_ALIGN == 0
        tile_rows += [r // ROW_ALIGN for r in range(lo, hi, bn)]
    tile_rows = jnp.asarray(tile_rows, jnp.int32)

    w_t = jnp.swapaxes(w_in, 1, 2).astype(BF16)
    half = D_ROPE // 2
    kr0 = src["kr"][0]
    w_small_t = jnp.concatenate([
        w_t[:, kr0:kr0 + D_ROPE], w_t[:, kr0 + half:kr0 + D_ROPE], w_t[:, kr0:kr0 + half],
        w_t[:, src["i"][0]:src["f"][1]],
        jnp.zeros((depth, LANES - 2 * h_a, d), BF16)], axis=1)

    gb_col = jnp.pad(gate_bias, ((0, 0), (0, LANES - 2 * h_a)))[:, None, :]

    w_a_b, w_b_b, w_out_b = w_a.astype(BF16), w_b.astype(BF16), w_out.astype(BF16)
    wuq_t = jnp.swapaxes(w_uq, 1, 2).astype(BF16)
    w_ukv4 = w_ukv.reshape(depth, kvl, h_b, D_NOPE + D_VB)
    wk = w_ukv4[..., :D_NOPE].reshape(depth, kvl, h_b * D_NOPE).astype(BF16)
    wv_t = jnp.swapaxes(w_ukv4[..., D_NOPE:].reshape(depth, kvl, h_b * D_VB), 1, 2).astype(BF16)
    qn_g = q_norm_g[:, :, None]
    kn_g = k_norm_g[:, None, :]
    kn_g_sw = jnp.concatenate([k_norm_g[:, D_NOPE + half:], k_norm_g[:, D_NOPE:D_NOPE + half]], axis=-1)[:, None, :]

    inv_freq = jnp.exp(-math.log(ROPE_THETA) * jnp.arange(0, D_ROPE, 2, dtype=F32) / D_ROPE)
    ang_t = inv_freq[None, :, None] * positions.astype(F32)[:, None, :]
    cos_t, sin_t = jnp.cos(ang_t), jnp.sin(ang_t)
    cos_k = jnp.swapaxes(jnp.concatenate([cos_t, cos_t], axis=1), 1, 2)
    sin_k = jnp.swapaxes(jnp.concatenate([-sin_t, sin_t], axis=1), 1, 2)

    x2 = x.reshape(batch * seq, d)
    for l in range(depth):
        proj, small = _inproj(x2, norm_g[l][None, :], w_t, tile_rows, w_small_t, layer=l, bn=bn)
        h_gated = _mlstm(proj, small, conv_w[l], gb_col[l], mlstm_norm_g[l][None, :],
                         batch=batch, seq=seq, n_heads=h_a, offs=offs)
        proj3 = proj.reshape(batch, seq, -1)
        small3 = small.reshape(batch, seq, -1)
        q_t, k, v_t = _mla_prep(proj3, small3, cos_t, sin_t, cos_k, sin_k, q_lat_g[l][None, :], kv_lat_g[l][None, :],
                                wuq_t, wk, wv_t, qn_g[l], kn_g[l], kn_g_sw[l], n_heads=h_b, offs=offs, layer=l)
        o_gated = _attention(q_t, k, v_t, proj3, offs=offs).reshape(batch * seq, d_b)
        y = _merge(h_gated, o_gated, w_a_b, w_b_b, proj, offs=offs, layer=l)
        x2 = _outproj(y, w_out_b, x2, layer=l)
    return x2.reshape(batch, seq, d)
```
